```python
import math, functools
import jax, jax.numpy as jnp
from jax import lax
import numpy as np

D_MODEL = 1024
BATCH = 16
SEQ = 2048
DEPTH = 2
DEC_BATCH = 32
DEC_SEQ = 4
PAST_LEN = 16384
PAGE_SIZE = 128

HEAD_DIM = 64
N_MIXERS = 4
N_HEADS = D_MODEL // HEAD_DIM
N_GROUP_HEADS = N_HEADS // N_MIXERS
MIX_WIDTH = N_HEADS * HEAD_DIM
DIFF_HALF = HEAD_DIM // 2
IDX_HEADS = 8
IDX_DIM = 64
DSA_TOPK = 256
Q_BLOCK = 128
ROPE_THETA = 10000.0
N_EXPERT_GROUPS = 4
EXPERTS_PER_GROUP = 8
N_EXPERTS = N_EXPERT_GROUPS * EXPERTS_PER_GROUP
EXPERT_TOPK = 2
D_EXPERT = D_MODEL // 2
MOE_BLOCK = 256
LN_EPS = 1e-5
DEEPNORM_ALPHA = (2 * DEPTH) ** 0.25
DEEPNORM_BETA = (8 * DEPTH) ** -0.25

Q_OFF = 0
K_OFF = MIX_WIDTH
V_OFF = 2 * MIX_WIDTH
QI_OFF = 3 * MIX_WIDTH
KI_OFF = QI_OFF + IDX_HEADS * IDX_DIM
WI_OFF = KI_OFF + IDX_DIM
F_OFF = WI_OFF + IDX_HEADS
IN_COLS = F_OFF + N_GROUP_HEADS

kernel_name = 'hymba_style_diff_dsa_stickbreak_fox_hmoe_step'


def rope(x, pos):
    half = x.shape[-1] // 2
    inv = ROPE_THETA ** (-jnp.arange(half, dtype=jnp.float32) / half)
    ang = pos.astype(jnp.float32)[:, None] * inv[None, :]
    cos = jnp.cos(ang)[None, :, None, :].astype(x.dtype)
    sin = jnp.sin(ang)[None, :, None, :].astype(x.dtype)
    x1, x2 = x[..., :half], x[..., half:]
    return jnp.concatenate([x1 * cos - x2 * sin, x2 * cos + x1 * sin], axis=-1)


def rope_heads(x, pos):
    b, s = x.shape[:2]
    g = N_GROUP_HEADS
    xa = rope(x[:, :, :g].reshape(b, s, 2 * g, DIFF_HALF), pos).reshape(b, s, g, HEAD_DIM)
    xb = rope(x[:, :, g:2 * g], pos)
    return jnp.concatenate([xa, xb, x[:, :, 2 * g:]], axis=2)


def layer_norm(x, g, b):
    x32 = x.astype(jnp.float32)
    mu = jnp.mean(x32, axis=-1, keepdims=True)
    var = jnp.mean(jnp.square(x32 - mu), axis=-1, keepdims=True)
    return ((x32 - mu) * lax.rsqrt(var + LN_EPS) * g + b).astype(x.dtype)


def project(x, pos, w_in, b_f):
    b, s, _ = x.shape
    p = jnp.einsum('bsd,dc->bsc', x, w_in)
    q = p[..., Q_OFF:K_OFF].reshape(b, s, N_HEADS, HEAD_DIM)
    k = p[..., K_OFF:V_OFF].reshape(b, s, N_HEADS, HEAD_DIM)
    v = p[..., V_OFF:QI_OFF].reshape(b, s, N_HEADS, HEAD_DIM)
    qi = p[..., QI_OFF:KI_OFF].reshape(b, s, IDX_HEADS, IDX_DIM)
    ki = p[..., KI_OFF:WI_OFF]
    wi = p[..., WI_OFF:F_OFF]
    logf = jax.nn.log_sigmoid((p[..., F_OFF:IN_COLS] + b_f).astype(jnp.float32))
    q = rope_heads(q, pos)
    k = rope_heads(k, pos)
    qi = rope(qi, pos)
    ki = rope(ki[:, :, None, :], pos)[:, :, 0]
    return q, k, v, qi, ki, wi, logf


def masked_softmax(s, mask):
    return jax.nn.softmax(jnp.where(mask, s, -jnp.inf), axis=-1)


def attend(q, qi, wi, fq, q_pos, k, v, ki, fk, k_pos, lam, lam_init, sub_g):
    g = N_GROUP_HEADS
    scale = HEAD_DIM ** -0.5
    n_keys = k.shape[1]
    causal = k_pos[None, :] <= q_pos[:, None]
    strict = k_pos[None, :] < q_pos[:, None]

    qa, ka, va = q[:, :, :g], k[:, :, :g], v[:, :, :g]
    s1 = jnp.einsum('bqhd,bkhd->bhqk', qa[..., :DIFF_HALF], ka[..., :DIFF_HALF]).astype(jnp.float32) * DIFF_HALF ** -0.5
    s2 = jnp.einsum('bqhd,bkhd->bhqk', qa[..., DIFF_HALF:], ka[..., DIFF_HALF:]).astype(jnp.float32) * DIFF_HALF ** -0.5
    wa = masked_softmax(s1, causal) - lam * masked_softmax(s2, causal)
    oa = jnp.einsum('bhqk,bkhd->bqhd', wa.astype(va.dtype), va).astype(jnp.float32)
    oa = oa * lax.rsqrt(jnp.mean(jnp.square(oa), axis=-1, keepdims=True) + LN_EPS) * sub_g * (1.0 - lam_init)
    oa = oa.astype(v.dtype)

    qb, kb, vb = q[:, :, g:2 * g], k[:, :, g:2 * g], v[:, :, g:2 * g]
    isc = jnp.einsum('bqhd,bkd->bqhk', qi, ki).astype(jnp.float32) * IDX_DIM ** -0.5
    isc = jnp.einsum('bqhk,bqh->bqk', jax.nn.relu(isc), wi.astype(jnp.float32) * IDX_HEADS ** -0.5)
    isc = jnp.where(causal[None], isc, -jnp.inf)
    n_sel = min(DSA_TOPK, n_keys // 4)
    _, sel = lax.top_k(isc, n_sel)
    take = jax.vmap(lambda a, i: a[i])
    ks = take(kb, sel)
    vs = take(vb, sel)
    valid = k_pos[sel] <= q_pos[None, :, None]
    lb = jnp.einsum('bqhd,bqkhd->bhqk', qb, ks).astype(jnp.float32) * scale
    pb = masked_softmax(lb, valid[:, None])
    ob = jnp.einsum('bhqk,bqkhd->bqhd', pb.astype(vs.dtype), vs)

    qc, kc, vc = q[:, :, 2 * g:3 * g], k[:, :, 2 * g:3 * g], v[:, :, 2 * g:3 * g]
    z = jnp.einsum('bqhd,bkhd->bhqk', qc, kc).astype(jnp.float32) * scale
    log_1m = jnp.where(strict, jax.nn.log_sigmoid(-z), 0.0)
    tail = lax.cumsum(log_1m, axis=3, reverse=True) - log_1m
    wc = jnp.where(strict, jnp.exp(jax.nn.log_sigmoid(z) + tail), 0.0)
    oc = jnp.einsum('bhqk,bkhd->bqhd', wc.astype(vc.dtype), vc)

    qd, kd, vd = q[:, :, 3 * g:], k[:, :, 3 * g:], v[:, :, 3 * g:]
    ld = jnp.einsum('bqhd,bkhd->bhqk', qd, kd).astype(jnp.float32) * scale
    ld = ld + jnp.swapaxes(fq, 1, 2)[..., None] - jnp.swapaxes(fk, 1, 2)[:, :, None, :]
    pd = masked_softmax(ld, causal)
    od = jnp.einsum('bhqk,bkhd->bqhd', pd.astype(vd.dtype), vd)

    return jnp.concatenate([oa, ob.astype(v.dtype), oc.astype(v.dtype), od.astype(v.dtype)], axis=2)


def moe(x, router_g, router_g_b, router_e, router_e_b, w_gate_up, w_down):
    t, d = x.shape
    lg = jnp.einsum('td,dg->tg', x, router_g).astype(jnp.float32) + router_g_b
    pg = jax.nn.softmax(lg, axis=-1)
    grp = jnp.argmax(lg, axis=-1)
    pg_top = jnp.take_along_axis(pg, grp[:, None], axis=-1)
    le = (jnp.einsum('td,de->te', x, router_e).astype(jnp.float32) + router_e_b)
    le = le.reshape(t, N_EXPERT_GROUPS, EXPERTS_PER_GROUP)
    le = jnp.take_along_axis(le, grp[:, None, None], axis=1)[:, 0]
    pe_top, ie_top = lax.top_k(jax.nn.softmax(le, axis=-1), EXPERT_TOPK)
    gate = pg_top * pe_top / jnp.sum(pe_top, axis=-1, keepdims=True)
    eid = grp[:, None].astype(jnp.int32) * EXPERTS_PER_GROUP + ie_top.astype(jnp.int32)
    n_assign = t * EXPERT_TOPK
    e_flat = eid.reshape(-1)
    tok = jnp.repeat(jnp.arange(t, dtype=jnp.int32), EXPERT_TOPK)
    order = jnp.argsort(e_flat, stable=True)
    e_s, tok_s, g_s = e_flat[order], tok[order], gate.reshape(-1)[order]
    counts = jax.ops.segment_sum(jnp.ones_like(e_flat), e_flat, num_segments=N_EXPERTS)
    starts = jnp.cumsum(counts) - counts
    padded = (counts + MOE_BLOCK - 1) // MOE_BLOCK * MOE_BLOCK
    pends = jnp.cumsum(padded)
    pstarts = pends - padded
    dest = pstarts[e_s] + jnp.arange(n_assign, dtype=jnp.int32) - starts[e_s]
    n_blocks = -(-n_assign // MOE_BLOCK) + N_EXPERTS
    buf = jnp.zeros((n_blocks * MOE_BLOCK, d), x.dtype).at[dest].set(x[tok_s])
    block_start = jnp.arange(n_blocks, dtype=jnp.int32) * MOE_BLOCK
    block_e = jnp.minimum(jnp.sum(block_start[:, None] >= pends[None, :], axis=-1), N_EXPERTS - 1)

    def run_block(args):
        xb, e = args
        h = xb @ w_gate_up[e]
        a, u = jnp.split(h, 2, axis=-1)
        return (jax.nn.silu(a) * u) @ w_down[e]

    out = lax.map(run_block, (buf.reshape(n_blocks, MOE_BLOCK, d), block_e)).reshape(n_blocks * MOE_BLOCK, d)
    contrib = (out[dest].astype(jnp.float32) * g_s[:, None]).astype(x.dtype)
    return jnp.zeros_like(x).at[tok_s].add(contrib)


def trunk_layer(x, pos, past, lam_init, w_in, b_f, lam_q1, lam_k1, lam_q2, lam_k2, sub_g, w_out,
                ln1_g, ln1_b, router_g, router_g_b, router_e, router_e_b, w_gate_up, w_down, ln2_g, ln2_b):
    b, nq, _ = x.shape
    q, k, v, qi, ki, wi, logf = project(x, pos, w_in, b_f)
    if past is None:
        kf, vf, kif, lff = k, v, ki, logf
    else:
        pk, pv, pidx, plf = past
        kf = jnp.concatenate([pk, k], axis=1)
        vf = jnp.concatenate([pv, v], axis=1)
        kif = jnp.concatenate([pidx, ki], axis=1)
        lff = jnp.concatenate([plf.astype(jnp.float32), logf], axis=1)
    n_keys = kf.shape[1]
    k_pos = jnp.arange(n_keys, dtype=jnp.int32)
    fcum = lax.cumsum(lff.astype(jnp.float32), axis=1)
    fq = fcum[:, n_keys - nq:]
    lam = (jnp.exp(jnp.sum(lam_q1 * lam_k1)) - jnp.exp(jnp.sum(lam_q2 * lam_k2))).astype(jnp.float32) + lam_init
    mix = functools.partial(attend, k=kf, v=vf, ki=kif, fk=fcum, k_pos=k_pos, lam=lam, lam_init=lam_init, sub_g=sub_g)
    if nq > Q_BLOCK and nq % Q_BLOCK == 0:
        nb = nq // Q_BLOCK
        to_blocks = lambda a: jnp.swapaxes(a.reshape(a.shape[0], nb, Q_BLOCK, *a.shape[2:]), 0, 1)
        o = lax.map(lambda tpl: mix(*tpl), (to_blocks(q), to_blocks(qi), to_blocks(wi), to_blocks(fq), pos.reshape(nb, Q_BLOCK)))
        o = jnp.swapaxes(o, 0, 1).reshape(b, nq, N_HEADS, HEAD_DIM)
    else:
        o = mix(q, qi, wi, fq, pos)
    o = jnp.einsum('bsc,cd->bsd', o.reshape(b, nq, MIX_WIDTH).astype(x.dtype), w_out)
    h = layer_norm(DEEPNORM_ALPHA * x + o, ln1_g, ln1_b)
    f = moe(h.reshape(b * nq, D_MODEL), router_g, router_g_b, router_e, router_e_b, w_gate_up, w_down).reshape(b, nq, D_MODEL)
    y = layer_norm(DEEPNORM_ALPHA * h + f, ln2_g, ln2_b)
    return y, (k, v, ki, logf)


def gather_past(cache, layer, page_table):
    g = cache[layer, page_table]
    return g.reshape(g.shape[0], g.shape[1] * g.shape[2], *g.shape[3:])


def setup_inputs(seed: int = 0) -> dict:
    key = jax.random.key(seed)
    ks = jax.random.split(key, 24)
    n_pages = PAST_LEN // PAGE_SIZE
    n_used = DEC_BATCH * n_pages
    n_pool = n_used + n_used // 4 + 1
    nrm = lambda kk, shape: jax.random.normal(kk, shape, jnp.float32)
    page_table = jax.random.permutation(ks[0], n_pool)[:n_used].reshape(DEC_BATCH, n_pages).astype(jnp.int32)
    return {
        'x_prompt': nrm(ks[1], (BATCH, SEQ, D_MODEL)),
        'x_sample': nrm(ks[2], (DEC_BATCH, DEC_SEQ, D_MODEL)),
        'cache_k': nrm(ks[3], (DEPTH, n_pool, PAGE_SIZE, N_HEADS, HEAD_DIM)),
        'cache_v': nrm(ks[4], (DEPTH, n_pool, PAGE_SIZE, N_HEADS, HEAD_DIM)),
        'cache_idx': nrm(ks[5], (DEPTH, n_pool, PAGE_SIZE, IDX_DIM)),
        'cache_logf': jax.nn.log_sigmoid(1.0 + nrm(ks[6], (DEPTH, n_pool, PAGE_SIZE, N_GROUP_HEADS))),
        'page_table': page_table,
        'w_in': nrm(ks[7], (DEPTH, D_MODEL, IN_COLS)) * D_MODEL ** -0.5,
        'b_f': 1.0 + 0.1 * nrm(ks[8], (DEPTH, N_GROUP_HEADS)),
        'lam_q1': 0.1 * nrm(ks[9], (DEPTH, DIFF_HALF)),
        'lam_k1': 0.1 * nrm(ks[10], (DEPTH, DIFF_HALF)),
        'lam_q2': 0.1 * nrm(ks[11], (DEPTH, DIFF_HALF)),
        'lam_k2': 0.1 * nrm(ks[12], (DEPTH, DIFF_HALF)),
        'sub_g': 1.0 + 0.02 * nrm(ks[13], (DEPTH, HEAD_DIM)),
        'w_out': nrm(ks[14], (DEPTH, MIX_WIDTH, D_MODEL)) * MIX_WIDTH ** -0.5 * DEEPNORM_BETA,
        'ln1_g': 1.0 + 0.02 * nrm(ks[15], (DEPTH, D_MODEL)),
        'ln1_b': 0.02 * nrm(ks[16], (DEPTH, D_MODEL)),
        'router_g': nrm(ks[17], (DEPTH, D_MODEL, N_EXPERT_GROUPS)) * D_MODEL ** -0.5,
        'router_g_b': 0.01 * nrm(ks[18], (DEPTH, N_EXPERT_GROUPS)),
        'router_e': nrm(ks[19], (DEPTH, D_MODEL, N_EXPERTS)) * D_MODEL ** -0.5,
        'router_e_b': 0.01 * nrm(ks[20], (DEPTH, N_EXPERTS)),
        'w_gate_up': nrm(ks[21], (DEPTH, N_EXPERTS, D_MODEL, 2 * D_EXPERT)) * D_MODEL ** -0.5,
        'w_down': nrm(ks[22], (DEPTH, N_EXPERTS, D_EXPERT, D_MODEL)) * D_EXPERT ** -0.5 * DEEPNORM_BETA,
        'ln2_g': 1.0 + 0.02 * nrm(ks[23], (DEPTH, D_MODEL)),
        'ln2_b': 0.02 * nrm(jax.random.fold_in(ks[23], 1), (DEPTH, D_MODEL)),
    }


def reference(x_prompt, x_sample, cache_k, cache_v, cache_idx, cache_logf, page_table,
              w_in, b_f, lam_q1, lam_k1, lam_q2, lam_k2, sub_g, w_out, ln1_g, ln1_b,
              router_g, router_g_b, router_e, router_e_b, w_gate_up, w_down, ln2_g, ln2_b):
    past_len = page_table.shape[1] * PAGE_SIZE
    pos_p = jnp.arange(x_prompt.shape[1], dtype=jnp.int32)
    pos_s = past_len + jnp.arange(x_sample.shape[1], dtype=jnp.int32)
    yp, ys = x_prompt, x_sample
    rows_p, rows_s = [], []
    for l in range(DEPTH):
        lam_init = 0.8 - 0.6 * math.exp(-0.3 * l)
        lw = (w_in[l], b_f[l], lam_q1[l], lam_k1[l], lam_q2[l], lam_k2[l], sub_g[l], w_out[l],
              ln1_g[l], ln1_b[l], router_g[l], router_g_b[l], router_e[l], router_e_b[l],
              w_gate_up[l], w_down[l], ln2_g[l], ln2_b[l])
        yp, rp = trunk_layer(yp, pos_p, None, lam_init, *lw)
        past = (gather_past(cache_k, l, page_table), gather_past(cache_v, l, page_table),
                gather_past(cache_idx, l, page_table), gather_past(cache_logf, l, page_table))
        ys, rs = trunk_layer(ys, pos_s, past, lam_init, *lw)
        rows_p.append(rp)
        rows_s.append(rs)
    new_k_prompt = jnp.stack([r[0] for r in rows_p])
    new_v_prompt = jnp.stack([r[1] for r in rows_p])
    new_idx_prompt = jnp.stack([r[2] for r in rows_p])
    new_logf_prompt = jnp.stack([r[3] for r in rows_p])
    new_k_sample = jnp.stack([r[0] for r in rows_s])
    new_v_sample = jnp.stack([r[1] for r in rows_s])
    new_idx_sample = jnp.stack([r[2] for r in rows_s])
    new_logf_sample = jnp.stack([r[3] for r in rows_s])
    return (yp, ys, new_k_prompt, new_v_prompt, new_idx_prompt, new_logf_prompt,
            new_k_sample, new_v_sample, new_idx_sample, new_logf_sample)
```

```python
import functools
import math

import jax
import jax.numpy as jnp
import numpy as np
from jax import lax
from jax.experimental import pallas as pl
from jax.experimental.pallas import tpu as pltpu

F32 = jnp.float32
BF16 = jnp.bfloat16
I32 = jnp.int32

HEAD_DIM = 64
N_HEADS = 16
N_GROUP_HEADS = 4
DIFF_HALF = 32
IDX_HEADS = 8
IDX_DIM = 64
DSA_TOPK = 256
ROPE_THETA = 10000.0
N_EXPERT_GROUPS = 4
EXPERTS_PER_GROUP = 8
N_EXPERTS = 32
LN_EPS = 1e-5

LANES = 128
VMEM_LIMIT = 56 * 1024 * 1024

TM = 256
TQ = 256
MOE_ROWS = 256
NEG = -1e30
INT_MIN = -2 ** 31

WI_LANE = IDX_DIM
F_LANE = IDX_DIM + IDX_HEADS


def _cparams(sem):
    return pltpu.CompilerParams(dimension_semantics=sem, vmem_limit_bytes=VMEM_LIMIT)


def _dot(a, b):
    return jnp.dot(a, b, preferred_element_type=F32)


def _dot_nt(a, b):
    return lax.dot_general(a, b, (((1,), (1,)), ((), ())), preferred_element_type=F32)


def _split2(x):
    hi = x.astype(BF16)
    lo = (x - hi.astype(F32)).astype(BF16)
    return hi, lo


def _split3(x):
    hi = x.astype(BF16)
    r = x - hi.astype(F32)
    mid = r.astype(BF16)
    lo = (r - mid.astype(F32)).astype(BF16)
    return hi, mid, lo


def _log_sigmoid(z):
    return jnp.minimum(z, 0.0) - jnp.log1p(jnp.exp(-jnp.abs(z)))


def _iota(shape, dim):
    return lax.broadcasted_iota(I32, shape, dim)


def _rope_chunk(y, cos, sin_signed, half):
    lane = _iota(y.shape, 1)
    first = (lane % (2 * half)) < half
    fwd = pltpu.roll(y, LANES - half, 1)
    bwd = pltpu.roll(y, half, 1)
    return y * cos + jnp.where(first, fwd, bwd) * sin_signed


def _proj_kernel(x_ref, w_ref, tab_ref, bf_ref, qs_ref, kb_ref, vb_ref, kf_ref, vf_ref,
                 qib_ref, misc_ref, ki2_ref):
    xb = x_ref[...].astype(BF16)
    cos_d, sin_d = tab_ref[:, 0:128], tab_ref[:, 128:256]
    cos_f, sin_f = tab_ref[:, 256:384], tab_ref[:, 384:512]
    mix = N_HEADS * HEAD_DIM

    def chunk(c):
        return _dot(xb, w_ref[:, c * LANES:(c + 1) * LANES])

    def roped(c, y):
        if c < 2:
            return _rope_chunk(y, cos_d, sin_d, DIFF_HALF // 2)
        if c < 4:
            return _rope_chunk(y, cos_f, sin_f, HEAD_DIM // 2)
        return y

    nch = mix // LANES
    for c in range(nch):
        q = roped(c, chunk(c))
        scale = DIFF_HALF ** -0.5 if c < 2 else HEAD_DIM ** -0.5
        qs_ref[:, c * LANES:(c + 1) * LANES] = (q * scale).astype(BF16)
        k = roped(c, chunk(nch + c))
        kf_ref[:, c * LANES:(c + 1) * LANES] = k
        kb_ref[:, c * LANES:(c + 1) * LANES] = k.astype(BF16)
        v = chunk(2 * nch + c)
        vf_ref[:, c * LANES:(c + 1) * LANES] = v
        vb_ref[:, c * LANES:(c + 1) * LANES] = v.astype(BF16)
    for c in range(IDX_HEADS * IDX_DIM // LANES):
        qi = _rope_chunk(chunk(3 * nch + c), cos_f, sin_f, IDX_DIM // 2)
        qib_ref[:, c * LANES:(c + 1) * LANES] = qi.astype(BF16)
    raw = chunk(3 * nch + IDX_HEADS * IDX_DIM // LANES)
    ki = _rope_chunk(raw, cos_f, sin_f, IDX_DIM // 2)
    lane = _iota(raw.shape, 1)
    wi_scale = IDX_DIM ** -0.5 * IDX_HEADS ** -0.5
    logf = _log_sigmoid(raw + bf_ref[...])
    misc = jnp.where(lane < IDX_DIM, ki,
                     jnp.where(lane < F_LANE, raw * wi_scale,
                               jnp.where(lane < F_LANE + N_GROUP_HEADS, logf, 0.0)))
    misc_ref[...] = misc
    ki2_ref[...] = jnp.where(lane < IDX_DIM, ki, pltpu.roll(ki, IDX_DIM, 1)).astype(BF16)


def _project(x, w_pad, tab, bf_row, n_prompt_tiles, tiles_per_seq):
    tp, d = x.shape
    mix = N_HEADS * HEAD_DIM
    nt = tp // TM

    def tab_map(t):
        return (jnp.where(t < n_prompt_tiles, t % tiles_per_seq, tiles_per_seq + t - n_prompt_tiles), 0)

    row = lambda t: (t, 0)
    fixed = lambda t: (0, 0)
    out_shapes = (
        jax.ShapeDtypeStruct((tp, mix), BF16), jax.ShapeDtypeStruct((tp, mix), BF16),
        jax.ShapeDtypeStruct((tp, mix), BF16), jax.ShapeDtypeStruct((tp, mix), F32),
        jax.ShapeDtypeStruct((tp, mix), F32), jax.ShapeDtypeStruct((tp, IDX_HEADS * IDX_DIM), BF16),
        jax.ShapeDtypeStruct((tp, LANES), F32), jax.ShapeDtypeStruct((tp, LANES), BF16))
    return pl.pallas_call(
        _proj_kernel,
        grid=(nt,),
        in_specs=[pl.BlockSpec((TM, d), row), pl.BlockSpec(w_pad.shape, fixed),
                  pl.BlockSpec((TM, 512), tab_map), pl.BlockSpec((1, LANES), fixed)],
        out_specs=(pl.BlockSpec((TM, mix), row), pl.BlockSpec((TM, mix), row), pl.BlockSpec((TM, mix), row),
                   pl.BlockSpec((TM, mix), row), pl.BlockSpec((TM, mix), row),
                   pl.BlockSpec((TM, IDX_HEADS * IDX_DIM), row), pl.BlockSpec((TM, LANES), row),
                   pl.BlockSpec((TM, LANES), row)),
        out_shape=out_shapes,
        compiler_params=_cparams(("parallel",)),
        name="proj",
    )(x, w_pad, tab, bf_row)


def _rope_tables(pos):
    def pattern(half):
        lane = jnp.arange(LANES)
        inv = ROPE_THETA ** (-(lane % half).astype(F32) / half)
        ang = pos.astype(F32)[:, None] * inv[None, :]
        sign = jnp.where((lane % (2 * half)) < half, -1.0, 1.0)
        return jnp.cos(ang), jnp.sin(ang) * sign[None, :]
    cd, sd = pattern(DIFF_HALF // 2)
    cf, sf = pattern(HEAD_DIM // 2)
    return jnp.concatenate([cd, sd, cf, sf], axis=1)


def _fcum_kernel(misc_ref, fcol_ref, ft_ref):
    s = misc_ref.shape[0]
    r = _iota((TQ, TQ), 0)
    c = _iota((TQ, TQ), 1)
    tri = (r >= c).astype(BF16)
    sel = (_iota((8, LANES), 1) == _iota((8, LANES), 0) + F_LANE).astype(BF16)
    carry = jnp.zeros((1, LANES), F32)
    for b in range(s // TQ):
        seg = misc_ref[b * TQ:(b + 1) * TQ, :]
        cs = carry
        for piece in _split3(seg):
            cs = cs + _dot(tri, piece)
        fcol_ref[b * TQ:(b + 1) * TQ, :] = cs
        tr = jnp.zeros((8, TQ), F32)
        for piece in _split3(cs):
            tr = tr + _dot_nt(sel, piece)
        ft_ref[b] = tr
        carry = cs[TQ - 1:TQ, :]


def _fcum(misc, n_seq, seq):
    nkb = seq // TQ
    return pl.pallas_call(
        _fcum_kernel,
        grid=(n_seq,),
        in_specs=[pl.BlockSpec((seq, LANES), lambda b: (b, 0))],
        out_specs=(pl.BlockSpec((seq, LANES), lambda b: (b, 0)),
                   pl.BlockSpec((nkb, 8, TQ), lambda b: (b, 0, 0))),
        out_shape=(jax.ShapeDtypeStruct((n_seq * seq, LANES), F32),
                   jax.ShapeDtypeStruct((n_seq * nkb, 8, TQ), F32)),
        compiler_params=_cparams(("parallel",)),
        name="fcum",
    )(misc)


def _half_mask(width, offset, size):
    lane = _iota((1, LANES), 1)
    return (lane >= offset) & (lane < offset + size)


def _masked(q, offset, size):
    return jnp.where(_half_mask(LANES, offset, size), q, jnp.zeros_like(q))


def _softmax_step(state, s, vp):
    m, l, acc = state
    m_new = jnp.maximum(m, jnp.max(s, axis=1, keepdims=True))
    alpha = jnp.exp(m - m_new)
    p = jnp.exp(s - m_new)
    l = alpha * l + jnp.sum(p, axis=1, keepdims=True)
    acc = alpha * acc + _dot(p.astype(BF16), vp)
    return m_new, l, acc


def _softmax_init():
    return (jnp.full((TQ, 1), NEG, F32), jnp.zeros((TQ, 1), F32), jnp.zeros((TQ, LANES), F32))


def _causal_bias(i, j, strict=False):
    row = _iota((TQ, TQ), 0) + i * TQ
    col = _iota((TQ, TQ), 1) + j * TQ
    ok = (col < row) if strict else (col <= row)
    return ok


def _merge_heads(o_even, o_odd):
    lane = _iota(o_even.shape, 1)
    return jnp.where(lane < HEAD_DIM, o_even, o_odd)


def _attn_diff_kernel(lam_init, q_ref, k_ref, v_ref, lamv_ref, subg_ref, o_ref):
    i = pl.program_id(1)
    lv = lamv_ref[...]
    lam = (jnp.exp(jnp.sum(lv[0:1] * lv[1:2], axis=1, keepdims=True))
           - jnp.exp(jnp.sum(lv[2:3] * lv[3:4], axis=1, keepdims=True))) + lam_init
    for pair in range(2):
        qp = q_ref[:, pair * LANES:(pair + 1) * LANES]
        outs = []
        for half in range(2):
            q1 = _masked(qp, half * HEAD_DIM, DIFF_HALF)
            q2 = _masked(qp, half * HEAD_DIM + DIFF_HALF, DIFF_HALF)

            def block(j, st, q1=q1, q2=q2, pair=pair):
                kp = k_ref[pl.ds(pl.multiple_of(j * TQ, TQ), TQ), pair * LANES:(pair + 1) * LANES]
                vp = v_ref[pl.ds(pl.multiple_of(j * TQ, TQ), TQ), pair * LANES:(pair + 1) * LANES]
                ok = _causal_bias(i, j)
                s1 = jnp.where(ok, _dot_nt(q1, kp), NEG)
                s2 = jnp.where(ok, _dot_nt(q2, kp), NEG)
                return _softmax_step(st[0], s1, vp), _softmax_step(st[1], s2, vp)

            st = lax.fori_loop(0, i + 1, block, (_softmax_init(), _softmax_init()))
            (_, l1, a1), (_, l2, a2) = st
            oa = a1 / l1 - lam * (a2 / l2)
            hm = _half_mask(LANES, half * HEAD_DIM, HEAD_DIM)
            ms = jnp.sum(jnp.where(hm, oa * oa, 0.0), axis=1, keepdims=True) / HEAD_DIM
            outs.append(oa * lax.rsqrt(ms + LN_EPS) * subg_ref[...] * (1.0 - lam_init))
        o_ref[:, pair * LANES:(pair + 1) * LANES] = _merge_heads(outs[0], outs[1]).astype(o_ref.dtype)


def _attn_fox_kernel(q_ref, k_ref, v_ref, fcol_ref, ft_ref, o_ref):
    i = pl.program_id(1)
    for pair in range(2):
        qp = q_ref[:, pair * LANES:(pair + 1) * LANES]
        outs = []
        for half in range(2):
            g = 2 * pair + half
            qh = _masked(qp, half * HEAD_DIM, HEAD_DIM)
            fq = fcol_ref[:, F_LANE + g:F_LANE + g + 1]

            def block(j, st, qh=qh, fq=fq, g=g, pair=pair):
                kp = k_ref[pl.ds(pl.multiple_of(j * TQ, TQ), TQ), pair * LANES:(pair + 1) * LANES]
                vp = v_ref[pl.ds(pl.multiple_of(j * TQ, TQ), TQ), pair * LANES:(pair + 1) * LANES]
                fk = ft_ref[j][g:g + 1, :]
                s = _dot_nt(qh, kp) + fq - fk
                s = jnp.where(_causal_bias(i, j), s, NEG)
                return _softmax_step(st, s, vp)

            _, l, acc = lax.fori_loop(0, i + 1, block, _softmax_init())
            outs.append(acc / l)
        o_ref[:, pair * LANES:(pair + 1) * LANES] = _merge_heads(outs[0], outs[1]).astype(o_ref.dtype)


def _attn_sb_kernel(q_ref, k_ref, v_ref, o_ref):
    i = pl.program_id(1)
    r = _iota((TQ, TQ), 0)
    c = _iota((TQ, TQ), 1)
    suffix = (r >= c).astype(BF16)
    for pair in range(2):
        qp = q_ref[:, pair * LANES:(pair + 1) * LANES]
        outs = []
        for half in range(2):
            qh = _masked(qp, half * HEAD_DIM, HEAD_DIM)

            def block(jj, st, qh=qh, pair=pair):
                carry, acc = st
                j = i - jj
                kp = k_ref[pl.ds(pl.multiple_of(j * TQ, TQ), TQ), pair * LANES:(pair + 1) * LANES]
                vp = v_ref[pl.ds(pl.multiple_of(j * TQ, TQ), TQ), pair * LANES:(pair + 1) * LANES]
                ok = _causal_bias(i, j, strict=True)
                z = _dot_nt(qh, kp)
                ls = _log_sigmoid(z)
                l1m = jnp.where(ok, ls - z, 0.0)
                hi, lo = _split2(l1m)
                incl = _dot(hi, suffix) + _dot(lo, suffix)
                tail = incl - l1m + carry
                w = jnp.where(ok, jnp.exp(ls + tail), 0.0)
                acc = acc + _dot(w.astype(BF16), vp)
                return carry + incl[:, 0:1], acc

            _, acc = lax.fori_loop(0, i + 1, block,
                                   (jnp.zeros((TQ, 1), F32), jnp.zeros((TQ, LANES), F32)))
            outs.append(acc)
        o_ref[:, pair * LANES:(pair + 1) * LANES] = _merge_heads(outs[0], outs[1]).astype(o_ref.dtype)


def _sort_key(x):
    bits = pltpu.bitcast(x, I32)
    return bits ^ ((bits >> 31) & jnp.int32(0x7FFFFFFF))


def _topk_threshold(count_ge, n_sel, shape):
    def body(b, tau):
        inc = lax.shift_left(jnp.int32(1), jnp.int32(31) - b)
        cand = tau + inc
        return jnp.where(count_ge(cand) >= n_sel, cand, tau)
    return lax.fori_loop(0, 32, body, jnp.full(shape, INT_MIN, I32))


def _attn_dsa_kernel(n_sel, q_ref, k_ref, v_ref, qi_ref, ki2_ref, misc_ref, o_ref, key_ref, sel_ref):
    i = pl.program_id(1)
    nb = i + 1

    def score_block(j, _):
        kip = ki2_ref[pl.ds(pl.multiple_of(j * TQ, TQ), TQ), :]
        isc = jnp.zeros((TQ, TQ), F32)
        for ih in range(IDX_HEADS):
            qm = _masked(qi_ref[:, (ih // 2) * LANES:(ih // 2 + 1) * LANES], (ih % 2) * IDX_DIM, IDX_DIM)
            w = misc_ref[:, WI_LANE + ih:WI_LANE + ih + 1]
            isc = isc + jnp.maximum(_dot_nt(qm, kip), 0.0) * w
        key_ref[j] = jnp.where(_causal_bias(i, j), _sort_key(isc), INT_MIN)
        return 0
    lax.fori_loop(0, nb, score_block, 0)

    def count(pred):
        def body(j, acc):
            return acc + jnp.sum(jnp.where(pred(key_ref[j]), 1.0, 0.0), axis=1, keepdims=True)
        return lax.fori_loop(0, nb, body, jnp.zeros((TQ, 1), F32))

    tau = _topk_threshold(lambda cand: count(lambda k: k >= cand), float(n_sel), (TQ, 1))
    need = float(n_sel) - count(lambda k: k > tau)

    r = _iota((TQ, TQ), 0)
    c = _iota((TQ, TQ), 1)
    before = (r < c).astype(BF16)

    def select_block(j, seen):
        key = key_ref[j]
        eq = (key == tau) & (key != INT_MIN)
        eqf = jnp.where(eq, 1.0, 0.0)
        prefix = _dot(eqf.astype(BF16), before) + seen
        sel = (key > tau) | (eq & (prefix < need))
        sel_ref[j] = jnp.where(sel, 0.0, NEG)
        return seen + jnp.sum(eqf, axis=1, keepdims=True)
    lax.fori_loop(0, nb, select_block, jnp.zeros((TQ, 1), F32))

    for pair in range(2):
        qp = q_ref[:, pair * LANES:(pair + 1) * LANES]
        outs = []
        for half in range(2):
            qh = _masked(qp, half * HEAD_DIM, HEAD_DIM)

            def block(j, st, qh=qh, pair=pair):
                kp = k_ref[pl.ds(pl.multiple_of(j * TQ, TQ), TQ), pair * LANES:(pair + 1) * LANES]
                vp = v_ref[pl.ds(pl.multiple_of(j * TQ, TQ), TQ), pair * LANES:(pair + 1) * LANES]
                return _softmax_step(st, _dot_nt(qh, kp) + sel_ref[j], vp)

            _, l, acc = lax.fori_loop(0, nb, block, _softmax_init())
            outs.append(acc / l)
        o_ref[:, pair * LANES:(pair + 1) * LANES] = _merge_heads(outs[0], outs[1]).astype(o_ref.dtype)


def _prompt_attention(lam_init, qs, kb, vb, qib, ki2b, misc, fcol, ft, lamv, subg2, n_seq, seq):
    nq = seq // TQ
    nkb = seq // TQ
    t = n_seq * seq
    grp = N_GROUP_HEADS * HEAD_DIM

    def qspec(g):
        return pl.BlockSpec((TQ, grp), lambda b, i: (b * nq + i, g))

    def kvspec(g):
        return pl.BlockSpec((seq, grp), lambda b, i: (b, g))

    ospec = pl.BlockSpec((TQ, grp), lambda b, i: (b * nq + i, 0))
    oshape = jax.ShapeDtypeStruct((t, grp), BF16)
    fixed = lambda b, i: (0, 0)
    rowblk = lambda b, i: (b * nq + i, 0)
    params = _cparams(("parallel", "arbitrary"))

    o_a = pl.pallas_call(
        functools.partial(_attn_diff_kernel, lam_init), grid=(n_seq, nq),
        in_specs=[qspec(0), kvspec(0), kvspec(0), pl.BlockSpec((8, LANES), fixed),
                  pl.BlockSpec((1, LANES), fixed)],
        out_specs=ospec, out_shape=oshape, compiler_params=params, name="attn_diff",
    )(qs, kb, vb, lamv, subg2)
    n_sel = min(DSA_TOPK, seq // 4)
    o_b = pl.pallas_call(
        functools.partial(_attn_dsa_kernel, n_sel), grid=(n_seq, nq),
        in_specs=[qspec(1), kvspec(1), kvspec(1),
                  pl.BlockSpec((TQ, IDX_HEADS * IDX_DIM), rowblk),
                  pl.BlockSpec((seq, LANES), lambda b, i: (b, 0)),
                  pl.BlockSpec((TQ, LANES), rowblk)],
        out_specs=ospec, out_shape=oshape,
        scratch_shapes=[pltpu.VMEM((nkb, TQ, TQ), I32), pltpu.VMEM((nkb, TQ, TQ), F32)],
        compiler_params=params, name="attn_dsa",
    )(qs, kb, vb, qib, ki2b, misc)
    o_c = pl.pallas_call(
        _attn_sb_kernel, grid=(n_seq, nq),
        in_specs=[qspec(2), kvspec(2), kvspec(2)],
        out_specs=ospec, out_shape=oshape, compiler_params=params, name="attn_sb",
    )(qs, kb, vb)
    o_d = pl.pallas_call(
        _attn_fox_kernel, grid=(n_seq, nq),
        in_specs=[qspec(3), kvspec(3), kvspec(3), pl.BlockSpec((TQ, LANES), rowblk),
                  pl.BlockSpec((nkb, 8, TQ), lambda b, i: (b, 0, 0))],
        out_specs=ospec, out_shape=oshape, compiler_params=params, name="attn_fox",
    )(qs, kb, vb, fcol, ft)
    return o_a, o_b, o_c, o_d


PAGE = 128
CP = 4
CK = CP * PAGE
T8 = 8
SLOT_A1, SLOT_A2, SLOT_B, SLOT_C, SLOT_D = range(5)
N_QROWS = 5 * N_GROUP_HEADS * T8


def _dec_index_kernel(n_sel, dec_seq, pt_ref, *refs):
    ki_refs = refs[:CP]
    qi_ref, w_ref, kin_ref, selp_ref, seln_ref, key_ref, keyn_ref = refs[CP:]
    c = pl.program_id(1)
    nch = pl.num_programs(1)

    def scores(kib):
        s = jnp.maximum(_dot_nt(qi_ref[0], kib), 0.0) * w_ref[0][:, 0:1]
        isc = jnp.zeros((T8, kib.shape[0]), F32)
        for ih in range(IDX_HEADS):
            isc = isc + s[ih * T8:(ih + 1) * T8, :]
        return isc

    kib = jnp.concatenate([r[0, 0] for r in ki_refs], axis=0).astype(BF16)
    key_ref[c] = _sort_key(scores(kib))

    @pl.when(c == nch - 1)
    def _():
        t8 = _iota((T8, PAGE), 0)
        j = _iota((T8, PAGE), 1)
        keyn = jnp.where((j <= t8) & (j < dec_seq), _sort_key(scores(kin_ref[0].astype(BF16))), INT_MIN)
        keyn_ref[...] = keyn
        n_blk = key_ref.shape[0]

        def count(pred):
            def body(b, acc):
                return acc + jnp.sum(jnp.where(pred(key_ref[b]), 1.0, 0.0), axis=1, keepdims=True)
            acc = jnp.sum(jnp.where(pred(keyn), 1.0, 0.0), axis=1, keepdims=True)
            return lax.fori_loop(0, n_blk, body, acc)

        tau = _topk_threshold(lambda cand: count(lambda k: k >= cand), float(n_sel), (T8, 1))
        need = float(n_sel) - count(lambda k: k > tau)
        before = (_iota((CK, CK), 0) < _iota((CK, CK), 1)).astype(BF16)

        def select(key, seen, tri):
            eq = (key == tau) & (key != INT_MIN)
            eqf = jnp.where(eq, 1.0, 0.0)
            prefix = _dot(eqf.astype(BF16), tri) + seen
            sel = (key > tau) | (eq & (prefix < need))
            return jnp.where(sel, 0.0, NEG), seen + jnp.sum(eqf, axis=1, keepdims=True)

        def body(b, seen):
            bias, seen = select(key_ref[b], seen, before)
            selp_ref[0, b] = bias
            return seen
        seen = lax.fori_loop(0, n_blk, body, jnp.zeros((T8, 1), F32))
        bias, _ = select(keyn, seen, before[:PAGE, :PAGE])
        seln_ref[0] = bias


def _dec_index(l, n_sel, dec_seq, pt_flat, cache_idx, qim, wv, ki_new, n_pages):
    db = qim.shape[0]
    nch = n_pages // CP

    def page_spec(u):
        return pl.BlockSpec((1, 1, PAGE, IDX_DIM), lambda b, c, pt: (l, pt[b * n_pages + c * CP + u], 0, 0))

    per_b = lambda b, c, pt: (b, 0, 0)
    return pl.pallas_call(
        functools.partial(_dec_index_kernel, n_sel, dec_seq),
        grid_spec=pltpu.PrefetchScalarGridSpec(
            num_scalar_prefetch=1,
            grid=(db, nch),
            in_specs=[page_spec(u) for u in range(CP)] + [
                pl.BlockSpec((1, IDX_HEADS * T8, IDX_DIM), per_b),
                pl.BlockSpec((1, IDX_HEADS * T8, LANES), per_b),
                pl.BlockSpec((1, PAGE, IDX_DIM), per_b)],
            out_specs=(pl.BlockSpec((1, nch, T8, CK), lambda b, c, pt: (b, 0, 0, 0)),
                       pl.BlockSpec((1, T8, PAGE), per_b)),
            scratch_shapes=[pltpu.VMEM((nch, T8, CK), I32), pltpu.VMEM((T8, PAGE), I32)]),
        out_shape=(jax.ShapeDtypeStruct((db, nch, T8, CK), F32), jax.ShapeDtypeStruct((db, T8, PAGE), F32)),
        compiler_params=_cparams(("parallel", "arbitrary")),
        name="dec_index",
    )(pt_flat, *([cache_idx] * CP), qim, wv, ki_new)


def _dec_fsuffix_kernel(pt_ref, *refs):
    lf_refs = refs[:CP]
    dsuf_ref, carry_ref = refs[CP:]
    c = pl.program_id(1)

    @pl.when(c == 0)
    def _():
        carry_ref[...] = jnp.zeros_like(carry_ref)

    lf = jnp.concatenate([r[0, 0] for r in lf_refs], axis=1)
    after = (_iota((CK, CK), 0) > _iota((CK, CK), 1)).astype(BF16)
    carry = carry_ref[:, 0:1]
    suf = carry
    for piece in _split3(lf):
        suf = suf + _dot(piece, after)
    dsuf_ref[0, 0] = suf
    carry_ref[...] = jnp.broadcast_to(carry + jnp.sum(lf, axis=1, keepdims=True), carry_ref.shape)


def _dec_fsuffix(l, pt_flat, logf_t, db, n_pages):
    nch = n_pages // CP

    def page_spec(u):
        return pl.BlockSpec((1, 1, 8, PAGE),
                            lambda b, c, pt: (l, pt[b * n_pages + (nch - 1 - c) * CP + u], 0, 0))

    return pl.pallas_call(
        _dec_fsuffix_kernel,
        grid_spec=pltpu.PrefetchScalarGridSpec(
            num_scalar_prefetch=1,
            grid=(db, nch),
            in_specs=[page_spec(u) for u in range(CP)],
            out_specs=pl.BlockSpec((1, 1, 8, CK), lambda b, c, pt: (b, nch - 1 - c, 0, 0)),
            scratch_shapes=[pltpu.VMEM((8, LANES), F32)]),
        out_shape=jax.ShapeDtypeStruct((db, nch, 8, CK), F32),
        compiler_params=_cparams(("parallel", "arbitrary")),
        name="dec_fsuffix",
    )(pt_flat, *([logf_t] * CP))


def _slot_rows(slot, hh):
    return (slot * N_GROUP_HEADS + hh) * T8


def _dec_attn_kernel(lam_init, dec_seq, pt_ref, *refs):
    k_refs = refs[:CP]
    v_refs = refs[CP:2 * CP]
    (q_ref, kn_ref, vn_ref, selp_ref, seln_ref, dsuf_ref, miscn_ref, lamv_ref, subg_ref,
     o_ref, kf_ref, vf_ref, m_ref, l_ref, cc_ref, acc_ref) = refs[2 * CP:]
    c = pl.program_id(1)
    nch = pl.num_programs(1)
    rows = _iota((N_QROWS, 1), 0)
    is_c = (rows >= _slot_rows(SLOT_C, 0)) & (rows < _slot_rows(SLOT_D, 0))
    c0, c1 = _slot_rows(SLOT_C, 0), _slot_rows(SLOT_D, 0)
    b0 = _slot_rows(SLOT_B, 0)
    q = q_ref[0]

    lfn = miscn_ref[0]
    t8 = _iota((T8, LANES), 0)
    cn = jnp.zeros((T8, LANES), F32)
    for j in range(dec_seq):
        cn = cn + jnp.where(t8 >= j, lfn[j:j + 1, :], 0.0)

    def update(s, bias, ok_c, kf, vf, suffix):
        m_old, l_old, carry = m_ref[:, 0:1], l_ref[:, 0:1], cc_ref[:, 0:1]
        sm = s + bias
        m_new = jnp.maximum(m_old, jnp.max(sm, axis=1, keepdims=True))
        alpha = jnp.exp(m_old - m_new)
        p = jnp.exp(sm - m_new)
        l_new = alpha * l_old + jnp.sum(p, axis=1, keepdims=True)
        z = s[c0:c1]
        ls = _log_sigmoid(z)
        l1m = jnp.where(ok_c, ls - z, 0.0)
        hi, lo = _split2(l1m)
        incl = _dot(hi, suffix) + _dot(lo, suffix)
        tail = incl - l1m + carry
        w = jnp.where(ok_c, jnp.exp(ls + tail), 0.0)
        pw = jnp.concatenate([p[:c0], w, p[c1:]], axis=0).astype(BF16)
        acc_ref[...] = jnp.where(is_c, 1.0, alpha) * acc_ref[...] + _dot(pw, vf)
        m_ref[...] = jnp.broadcast_to(m_new, m_ref.shape)
        l_ref[...] = jnp.broadcast_to(l_new, l_ref.shape)
        cc_ref[...] = jnp.broadcast_to(carry + incl[:, 0:1], cc_ref.shape)

    @pl.when(c == 0)
    def _():
        m_ref[...] = jnp.full(m_ref.shape, NEG, F32)
        l_ref[...] = jnp.zeros_like(l_ref)
        cc_ref[...] = jnp.zeros_like(cc_ref)
        acc_ref[...] = jnp.zeros_like(acc_ref)
        kn, vn = kn_ref[0], vn_ref[0]
        s = _dot_nt(q, kn)
        jn = _iota((N_QROWS, PAGE), 1)
        tn = _iota((N_QROWS, PAGE), 0) % T8
        live = jn < dec_seq
        sel_g = (_iota((8, LANES), 1) == _iota((8, LANES), 0) + F_LANE).astype(BF16)
        cn_pad = jnp.concatenate([cn, jnp.zeros((PAGE - T8, LANES), F32)], axis=0)
        cnt = jnp.zeros((8, PAGE), F32)
        for piece in _split3(cn_pad):
            cnt = cnt + _dot_nt(sel_g, piece)
        blocks = [jnp.zeros((b0, PAGE), F32)]
        blocks += [seln_ref[0]] * N_GROUP_HEADS
        blocks += [jnp.zeros((c1 - c0, PAGE), F32)]
        for g in range(N_GROUP_HEADS):
            blocks.append(cn[:, F_LANE + g:F_LANE + g + 1] - cnt[g:g + 1, :])
        bias = jnp.where(live & (jn <= tn), jnp.concatenate(blocks, axis=0), NEG)
        ok_c = (live & (jn < tn))[c0:c1]
        suffix = (_iota((PAGE, PAGE), 0) >= _iota((PAGE, PAGE), 1)).astype(BF16)
        update(s, bias, ok_c, kn, vn, suffix)

    for u in range(CP):
        for hp in range(N_HEADS // 2):
            for src, dst in ((k_refs[u], kf_ref), (v_refs[u], vf_ref)):
                pair = jnp.concatenate([src[0, 0, :, 2 * hp, :], src[0, 0, :, 2 * hp + 1, :]], axis=1)
                dst[u * PAGE:(u + 1) * PAGE, hp * LANES:(hp + 1) * LANES] = pair.astype(BF16)
    kf, vf = kf_ref[...], vf_ref[...]
    s = _dot_nt(q, kf)
    blocks = [jnp.zeros((b0, CK), F32)]
    blocks += [selp_ref[0, 0]] * N_GROUP_HEADS
    blocks += [jnp.zeros((c1 - c0, CK), F32)]
    dsuf = dsuf_ref[0, 0]
    for g in range(N_GROUP_HEADS):
        blocks.append(cn[:, F_LANE + g:F_LANE + g + 1] + dsuf[g:g + 1, :])
    bias = jnp.concatenate(blocks, axis=0)
    suffix = (_iota((CK, CK), 0) >= _iota((CK, CK), 1)).astype(BF16)
    update(s, bias, jnp.full((c1 - c0, CK), True), kf, vf, suffix)

    @pl.when(c == nch - 1)
    def _():
        lv = lamv_ref[...]
        lam = (jnp.exp(jnp.sum(lv[0:1] * lv[1:2], axis=1, keepdims=True))
               - jnp.exp(jnp.sum(lv[2:3] * lv[3:4], axis=1, keepdims=True))) + lam_init
        o_all = acc_ref[...] / jnp.where(is_c, 1.0, l_ref[:, 0:1])
        lane = _iota((T8, N_HEADS * HEAD_DIM), 1)
        out = jnp.zeros((T8, N_HEADS * HEAD_DIM), F32)
        for hh in range(N_GROUP_HEADS):
            r1, r2 = _slot_rows(SLOT_A1, hh), _slot_rows(SLOT_A2, hh)
            oa = o_all[r1:r1 + T8] - lam * o_all[r2:r2 + T8]
            hm = (lane >= hh * HEAD_DIM) & (lane < (hh + 1) * HEAD_DIM)
            ms = jnp.sum(jnp.where(hm, oa * oa, 0.0), axis=1, keepdims=True) / HEAD_DIM
            oa = oa * lax.rsqrt(ms + LN_EPS) * subg_ref[...] * (1.0 - lam_init)
            out = jnp.where(hm, oa, out)
            for grp, slot in ((1, SLOT_B), (2, SLOT_C), (3, SLOT_D)):
                r = _slot_rows(slot, hh)
                head = grp * N_GROUP_HEADS + hh
                hm = (lane >= head * HEAD_DIM) & (lane < (head + 1) * HEAD_DIM)
                out = jnp.where(hm, o_all[r:r + T8], out)
        o_ref[0] = out.astype(o_ref.dtype)


def _dec_attention(l, lam_init, dec_seq, pt_flat, cache_k, cache_v, qall, k_new, v_new, selp, seln, dsuf,
                   misc_new, lamv, subg16, n_pages):
    db = qall.shape[0]
    nch = n_pages // CP
    mix = N_HEADS * HEAD_DIM

    def page_spec(u):
        return pl.BlockSpec((1, 1, PAGE, N_HEADS, HEAD_DIM),
                            lambda b, c, pt: (l, pt[b * n_pages + (nch - 1 - c) * CP + u], 0, 0, 0))

    per_b = lambda b, c, pt: (b, 0, 0)
    chunk = lambda b, c, pt: (b, nch - 1 - c, 0, 0)
    fixed = lambda b, c, pt: (0, 0)
    return pl.pallas_call(
        functools.partial(_dec_attn_kernel, lam_init, dec_seq),
        grid_spec=pltpu.PrefetchScalarGridSpec(
            num_scalar_prefetch=1,
            grid=(db, nch),
            in_specs=[page_spec(u) for u in range(CP)] * 2 + [
                pl.BlockSpec((1, N_QROWS, mix), per_b),
                pl.BlockSpec((1, PAGE, mix), per_b), pl.BlockSpec((1, PAGE, mix), per_b),
                pl.BlockSpec((1, 1, T8, CK), chunk), pl.BlockSpec((1, T8, PAGE), per_b),
                pl.BlockSpec((1, 1, 8, CK), chunk), pl.BlockSpec((1, T8, LANES), per_b),
                pl.BlockSpec((8, LANES), fixed), pl.BlockSpec((1, mix), fixed)],
            out_specs=pl.BlockSpec((1, T8, mix), per_b),
            scratch_shapes=[pltpu.VMEM((CK, mix), BF16), pltpu.VMEM((CK, mix), BF16),
                            pltpu.VMEM((N_QROWS, LANES), F32), pltpu.VMEM((N_QROWS, LANES), F32),
                            pltpu.VMEM((_slot_rows(SLOT_D, 0) - _slot_rows(SLOT_C, 0), LANES), F32),
                            pltpu.VMEM((N_QROWS, mix), F32)]),
        out_shape=jax.ShapeDtypeStruct((db, T8, mix), BF16),
        compiler_params=_cparams(("parallel", "arbitrary")),
        name="dec_attn",
    )(pt_flat, *([cache_k] * CP), *([cache_v] * CP), qall, k_new, v_new, selp, seln, dsuf, misc_new, lamv, subg16)


def _slot_lane_masks():
    mix = N_HEADS * HEAD_DIM
    m = np.zeros((5 * N_GROUP_HEADS, mix), np.float32)
    for hh in range(N_GROUP_HEADS):
        m[SLOT_A1 * 4 + hh, hh * HEAD_DIM:hh * HEAD_DIM + DIFF_HALF] = 1
        m[SLOT_A2 * 4 + hh, hh * HEAD_DIM + DIFF_HALF:(hh + 1) * HEAD_DIM] = 1
        for grp, slot in ((1, SLOT_B), (2, SLOT_C), (3, SLOT_D)):
            head = grp * N_GROUP_HEADS + hh
            m[slot * 4 + hh, head * HEAD_DIM:(head + 1) * HEAD_DIM] = 1
    return m


def _sample_attention(l, lam_init, p, rows, lamv, subg2, db, dec_seq):
    qs, kb, vb, qib, misc = rows
    n_s = db * dec_seq
    mix = N_HEADS * HEAD_DIM
    page_table = p["page_table"]
    n_pages = page_table.shape[1]
    pt_flat = page_table.reshape(-1).astype(I32)
    n_keys = n_pages * PAGE + dec_seq
    n_sel = min(DSA_TOPK, n_keys // 4)

    def per_seq(a, pad_to):
        a = a[:n_s].reshape(db, dec_seq, a.shape[-1])
        return jnp.pad(a, ((0, 0), (0, pad_to - dec_seq), (0, 0)))

    misc8 = per_seq(misc, T8)
    qi8 = per_seq(qib, T8).reshape(db, T8, IDX_HEADS, IDX_DIM)
    qim = jnp.swapaxes(qi8, 1, 2).reshape(db, IDX_HEADS * T8, IDX_DIM)
    wv = jnp.swapaxes(misc8[:, :, WI_LANE:WI_LANE + IDX_HEADS], 1, 2).reshape(db, IDX_HEADS * T8, 1)
    wv = jnp.broadcast_to(wv, (db, IDX_HEADS * T8, LANES))
    ki_new = per_seq(misc, PAGE)[:, :, :IDX_DIM]
    selp, seln = _dec_index(l, n_sel, dec_seq, pt_flat, p["cache_idx"], qim, wv, ki_new, n_pages)

    dsuf = _dec_fsuffix(l, pt_flat, p["cache_logf"], db, n_pages)

    masks = jnp.asarray(_slot_lane_masks(), BF16)
    q8 = per_seq(qs, T8)
    qall = (q8[:, None, :, :] * masks[None, :, None, :]).reshape(db, N_QROWS, mix)
    subg16 = jnp.tile(subg2, (1, mix // LANES))
    o8 = _dec_attention(l, lam_init, dec_seq, pt_flat, p["cache_k"], p["cache_v"], qall,
                        per_seq(kb, PAGE), per_seq(vb, PAGE), selp, seln, dsuf, misc8, lamv, subg16, n_pages)
    return o8[:, :dec_seq].reshape(n_s, mix)


E_LANE = N_EXPERT_GROUPS
BIG_LANE = 4096


def _layer_norm(x, g, b):
    mu = jnp.mean(x, axis=1, keepdims=True)
    xc = x - mu
    var = jnp.mean(xc * xc, axis=1, keepdims=True)
    return xc * lax.rsqrt(var + LN_EPS) * g + b


def _first_lane(mask, lane):
    return jnp.min(jnp.where(mask, lane, BIG_LANE), axis=1, keepdims=True)


def _mix_out_kernel(alpha, oa_ref, ob_ref, oc_ref, od_ref, x_ref, wo_ref, g_ref, b_ref, wr_ref, rb_ref,
                    h_ref, route_ref, cnt_ref, carry_ref):
    t = pl.program_id(0)
    grp_w = N_GROUP_HEADS * HEAD_DIM
    acc = x_ref[...] * alpha
    for n, o_ref in enumerate((oa_ref, ob_ref, oc_ref, od_ref)):
        acc = acc + _dot(o_ref[...], wo_ref[n * grp_w:(n + 1) * grp_w, :])
    h = _layer_norm(acc, g_ref[...], b_ref[...])
    h_ref[...] = h

    h_hi, h_lo = _split2(h)
    lg = (_dot(h_hi, wr_ref[0]) + _dot(h_lo, wr_ref[0]) + _dot(h_hi, wr_ref[1])) + rb_ref[...]
    lane = _iota(lg.shape, 1)
    is_g = lane < E_LANE
    is_e = (lane >= E_LANE) & (lane < E_LANE + N_EXPERTS)
    gmax = jnp.max(jnp.where(is_g, lg, NEG), axis=1, keepdims=True)
    grp = _first_lane(is_g & (lg == gmax), lane)
    pg_top = 1.0 / jnp.sum(jnp.where(is_g, jnp.exp(lg - gmax), 0.0), axis=1, keepdims=True)
    in_grp = is_e & (((lane - E_LANE) // EXPERTS_PER_GROUP) == grp)
    emax = jnp.max(jnp.where(in_grp, lg, NEG), axis=1, keepdims=True)
    ex = jnp.where(in_grp, jnp.exp(lg - emax), 0.0)
    pe = ex / jnp.sum(ex, axis=1, keepdims=True)
    p0 = jnp.max(jnp.where(in_grp, pe, -1.0), axis=1, keepdims=True)
    i0 = _first_lane(in_grp & (pe == p0), lane)
    rest = in_grp & (lane != i0)
    p1 = jnp.max(jnp.where(rest, pe, -1.0), axis=1, keepdims=True)
    i1 = _first_lane(rest & (pe == p1), lane)
    g0 = pg_top * p0 / (p0 + p1)
    g1 = pg_top * p1 / (p0 + p1)

    @pl.when(t == 0)
    def _():
        carry_ref[...] = jnp.zeros_like(carry_ref)
    onehot = jnp.where((lane == i0) | (lane == i1), 1.0, 0.0)
    r = _iota((TM, TM), 0)
    c = _iota((TM, TM), 1)
    before = (c < r).astype(BF16)
    prefix = _dot(before, onehot.astype(BF16)) + carry_ref[0:1, :]
    rank0 = jnp.sum(jnp.where(lane == i0, prefix, 0.0), axis=1, keepdims=True)
    rank1 = jnp.sum(jnp.where(lane == i1, prefix, 0.0), axis=1, keepdims=True)
    total = carry_ref[0:1, :] + jnp.sum(onehot, axis=0, keepdims=True)
    carry_ref[...] = jnp.broadcast_to(total, carry_ref.shape)
    cnt_ref[...] = jnp.broadcast_to(total, cnt_ref.shape)

    vals = (i0.astype(F32) - E_LANE, i1.astype(F32) - E_LANE, g0, g1, rank0, rank1)
    route = jnp.zeros(lg.shape, F32)
    for n, val in enumerate(vals):
        route = jnp.where(lane == n, val, route)
    route_ref[...] = route


def _mix_out(alpha, o4, x, w_out, ln_g, ln_b, wr, rb):
    tp, d = x.shape
    grp_w = N_GROUP_HEADS * HEAD_DIM
    row = lambda t: (t, 0)
    fixed = lambda t: (0, 0)
    return pl.pallas_call(
        functools.partial(_mix_out_kernel, alpha),
        grid=(tp // TM,),
        in_specs=[pl.BlockSpec((TM, grp_w), row)] * 4 + [
            pl.BlockSpec((TM, d), row), pl.BlockSpec(w_out.shape, fixed),
            pl.BlockSpec((1, d), fixed), pl.BlockSpec((1, d), fixed),
            pl.BlockSpec(wr.shape, lambda t: (0, 0, 0)), pl.BlockSpec((1, LANES), fixed)],
        out_specs=(pl.BlockSpec((TM, d), row), pl.BlockSpec((TM, LANES), row), pl.BlockSpec((8, LANES), fixed)),
        out_shape=(jax.ShapeDtypeStruct((tp, d), F32), jax.ShapeDtypeStruct((tp, LANES), F32),
                   jax.ShapeDtypeStruct((8, LANES), F32)),
        scratch_shapes=[pltpu.VMEM((8, LANES), F32)],
        compiler_params=_cparams(("arbitrary",)),
        name="mix_out",
    )(*o4, x, w_out, ln_g, ln_b, wr, rb)


def _dispatch_kernel(dest_ref, h_ref, buf_in_ref, buf_ref, sem):
    del buf_in_ref
    t = pl.program_id(0)

    def copy(r, k):
        d = dest_ref[(t * TM + r) * 2 + k]
        return pltpu.make_async_copy(h_ref.at[pl.ds(r, 1), :], buf_ref.at[pl.ds(d, 1), :], sem)

    def start(r, _):
        copy(r, 0).start()
        copy(r, 1).start()
        return 0

    def wait(r, _):
        copy(r, 0).wait()
        copy(r, 1).wait()
        return 0

    lax.fori_loop(0, TM, start, 0)
    lax.fori_loop(0, TM, wait, 0)


def _dispatch(dest_flat, h, n_rows):
    tp, d = h.shape
    buf0 = jnp.zeros((n_rows, d), F32)
    return pl.pallas_call(
        _dispatch_kernel,
        grid_spec=pltpu.PrefetchScalarGridSpec(
            num_scalar_prefetch=1,
            grid=(tp // TM,),
            in_specs=[pl.BlockSpec((TM, d), lambda t, dest: (t, 0)), pl.BlockSpec(memory_space=pl.ANY)],
            out_specs=pl.BlockSpec(memory_space=pl.ANY),
            scratch_shapes=[pltpu.SemaphoreType.DMA(())]),
        out_shape=jax.ShapeDtypeStruct((n_rows, d), F32),
        input_output_aliases={2: 0},
        compiler_params=_cparams(("arbitrary",)),
        name="moe_dispatch",
    )(dest_flat, h, buf0)


def _expert_kernel(be_ref, nb_ref, x_ref, wgu_ref, wd_ref, o_ref):
    j = pl.program_id(0)

    @pl.when(j < nb_ref[0])
    def _():
        de = wd_ref.shape[1]
        xb = x_ref[...].astype(BF16)
        hid = _dot(xb, wgu_ref[0].astype(BF16))
        a, u = hid[:, :de], hid[:, de:]
        act = (a * (1.0 / (1.0 + jnp.exp(-a)))) * u
        o_ref[...] = _dot(act.astype(BF16), wd_ref[0].astype(BF16))

    @pl.when(j >= nb_ref[0])
    def _():
        o_ref[...] = jnp.zeros_like(o_ref)


def _experts(block_e, n_used, buf, w_gate_up, w_down):
    n_rows, d = buf.shape
    _, _, de2 = w_gate_up.shape
    return pl.pallas_call(
        _expert_kernel,
        grid_spec=pltpu.PrefetchScalarGridSpec(
            num_scalar_prefetch=2,
            grid=(n_rows // MOE_ROWS,),
            in_specs=[pl.BlockSpec((MOE_ROWS, d), lambda j, be, nb: (j, 0)),
                      pl.BlockSpec((1, d, de2), lambda j, be, nb: (be[j], 0, 0)),
                      pl.BlockSpec((1, de2 // 2, d), lambda j, be, nb: (be[j], 0, 0))],
            out_specs=pl.BlockSpec((MOE_ROWS, d), lambda j, be, nb: (j, 0))),
        out_shape=jax.ShapeDtypeStruct((n_rows, d), F32),
        compiler_params=_cparams(("arbitrary",)),
        name="moe_experts",
    )(block_e, n_used, buf, w_gate_up, w_down)


def _combine_kernel(alpha, dest_ref, h_ref, route_ref, g_ref, b_ref, ebuf_ref, y_ref, rows_ref, sem):
    t = pl.program_id(0)

    def copy(r, k):
        d = dest_ref[(t * TM + r) * 2 + k]
        return pltpu.make_async_copy(ebuf_ref.at[pl.ds(d, 1), :], rows_ref.at[k, pl.ds(r, 1), :], sem)

    def start(r, _):
        copy(r, 0).start()
        copy(r, 1).start()
        return 0

    def wait(r, _):
        copy(r, 0).wait()
        copy(r, 1).wait()
        return 0

    lax.fori_loop(0, TM, start, 0)
    lax.fori_loop(0, TM, wait, 0)
    route = route_ref[...]
    f = rows_ref[0] * route[:, 2:3] + rows_ref[1] * route[:, 3:4]
    y_ref[...] = _layer_norm(h_ref[...] * alpha + f, g_ref[...], b_ref[...])


def _combine(alpha, dest_flat, h, route, ln_g, ln_b, ebuf):
    tp, d = h.shape
    row = lambda t, dest: (t, 0)
    fixed = lambda t, dest: (0, 0)
    return pl.pallas_call(
        functools.partial(_combine_kernel, alpha),
        grid_spec=pltpu.PrefetchScalarGridSpec(
            num_scalar_prefetch=1,
            grid=(tp // TM,),
            in_specs=[pl.BlockSpec((TM, d), row), pl.BlockSpec((TM, LANES), row),
                      pl.BlockSpec((1, d), fixed), pl.BlockSpec((1, d), fixed),
                      pl.BlockSpec(memory_space=pl.ANY)],
            out_specs=pl.BlockSpec((TM, d), row),
            scratch_shapes=[pltpu.VMEM((2, TM, d), F32), pltpu.SemaphoreType.DMA(())]),
        out_shape=jax.ShapeDtypeStruct((tp, d), F32),
        compiler_params=_cparams(("arbitrary",)),
        name="moe_combine",
    )(dest_flat, h, route, ln_g, ln_b, ebuf)


def _moe(alpha, h, route, counts, w_gate_up, w_down, ln_g, ln_b):
    tp, _ = h.shape
    eid = route[:, 0:2].astype(I32)
    rank = route[:, 4:6].astype(I32)
    counts = counts.astype(I32)
    padded = (counts + MOE_ROWS - 1) // MOE_ROWS * MOE_ROWS
    pends = jnp.cumsum(padded)
    pstarts = pends - padded
    dest = (pstarts[eid] + rank).reshape(-1)
    n_blocks = -(-(tp * 2) // MOE_ROWS) + N_EXPERTS
    block_start = jnp.arange(n_blocks, dtype=I32) * MOE_ROWS
    block_e = jnp.minimum(jnp.sum(block_start[:, None] >= pends[None, :], axis=-1), N_EXPERTS - 1).astype(I32)
    n_used = (pends[-1:] // MOE_ROWS).astype(I32)
    buf = _dispatch(dest, h, n_blocks * MOE_ROWS)
    ebuf = _experts(block_e, n_used, buf, w_gate_up, w_down)
    return _combine(alpha, dest, h, route, ln_g, ln_b, ebuf)


def _round_up(n, m):
    return -(-n // m) * m


def _concat_rows(x_prompt, x_sample):
    d = x_prompt.shape[-1]
    xp = x_prompt.reshape(-1, d)
    xs = x_sample.reshape(-1, d)
    n = xp.shape[0] + xs.shape[0]
    pad = _round_up(n, TM) - n
    return jnp.concatenate([xp, xs, jnp.zeros((pad, d), xp.dtype)], axis=0)


def _tables(seq, dec_seq, past_len, dec_batch):
    n_s = dec_batch * dec_seq
    pos_s = past_len + (jnp.arange(_round_up(n_s, TM), dtype=I32) % dec_seq)
    return jnp.concatenate([_rope_tables(jnp.arange(seq, dtype=I32)), _rope_tables(pos_s)], axis=0)


def _prep_layer(p, l):
    w_in = p["w_in"][l]
    cols = w_in.shape[1]
    w_pad = jnp.pad(w_in, ((0, 0), (0, _round_up(cols, LANES) - cols))).astype(BF16)
    bf_row = jnp.zeros((1, LANES), F32).at[0, F_LANE:F_LANE + N_GROUP_HEADS].set(p["b_f"][l])
    lamv = jnp.zeros((8, LANES), F32)
    for n, name in enumerate(("lam_q1", "lam_k1", "lam_q2", "lam_k2")):
        lamv = lamv.at[n, :DIFF_HALF].set(p[name][l])
    subg2 = jnp.concatenate([p["sub_g"][l], p["sub_g"][l]])[None, :]
    wr = jnp.zeros((p["router_g"].shape[1], LANES), F32)
    wr = wr.at[:, :E_LANE].set(p["router_g"][l]).at[:, E_LANE:E_LANE + N_EXPERTS].set(p["router_e"][l])
    wr_hi = wr.astype(BF16)
    wr_lo = (wr - wr_hi.astype(F32)).astype(BF16)
    rb = jnp.zeros((1, LANES), F32)
    rb = rb.at[0, :E_LANE].set(p["router_g_b"][l]).at[0, E_LANE:E_LANE + N_EXPERTS].set(p["router_e_b"][l])
    return dict(w_in=w_pad, bf_row=bf_row, lamv=lamv, subg2=subg2, wr=jnp.stack([wr_hi, wr_lo]), rb=rb,
                w_out=p["w_out"][l].astype(BF16))


def kernel(x_prompt, x_sample, cache_k, cache_v, cache_idx, cache_logf, page_table, w_in, b_f, lam_q1, lam_k1,
           lam_q2, lam_k2, sub_g, w_out, ln1_g, ln1_b, router_g, router_g_b, router_e, router_e_b, w_gate_up,
           w_down, ln2_g, ln2_b):
    p = dict(cache_k=cache_k, cache_v=cache_v, cache_idx=cache_idx, cache_logf=cache_logf,
             page_table=page_table, w_in=w_in, b_f=b_f, lam_q1=lam_q1, lam_k1=lam_k1, lam_q2=lam_q2,
             lam_k2=lam_k2, sub_g=sub_g, w_out=w_out, router_g=router_g, router_g_b=router_g_b,
             router_e=router_e, router_e_b=router_e_b)
    depth = w_in.shape[0]
    n_seq, seq, d = x_prompt.shape
    db, dec_seq, _ = x_sample.shape
    t, n_s = n_seq * seq, db * dec_seq
    assert seq % TQ == 0 and t % TM == 0 and n_s <= TM and dec_seq <= T8
    assert page_table.shape[1] % CP == 0 and cache_k.shape[2] == PAGE
    alpha = (2 * depth) ** 0.25
    past_len = page_table.shape[1] * PAGE
    tabs = _tables(seq, dec_seq, past_len, db)
    p["cache_logf"] = jnp.pad(jnp.swapaxes(cache_logf, 2, 3), ((0, 0), (0, 0), (0, 8 - N_GROUP_HEADS), (0, 0)))

    y = _concat_rows(x_prompt, x_sample)
    tp = y.shape[0]
    rows_p, rows_s = [], []
    for l in range(depth):
        lam_init = 0.8 - 0.6 * math.exp(-0.3 * l)
        st = _prep_layer(p, l)
        qs, kb, vb, kf, vf, qib, misc, ki2b = _project(y, st["w_in"], tabs, st["bf_row"], t // TM, seq // TM)
        fcol, ft = _fcum(misc, n_seq, seq)
        o_p = _prompt_attention(lam_init, qs, kb, vb, qib, ki2b, misc, fcol, ft, st["lamv"], st["subg2"],
                                n_seq, seq)
        o_s = _sample_attention(l, lam_init, p, tuple(a[t:] for a in (qs, kb, vb, qib, misc)),
                                st["lamv"], st["subg2"], db, dec_seq)
        grp_w = N_GROUP_HEADS * HEAD_DIM
        o4 = [jnp.concatenate([o_p[g], o_s[:, g * grp_w:(g + 1) * grp_w],
                               jnp.zeros((tp - t - n_s, grp_w), BF16)], axis=0) for g in range(4)]
        h, route, cnt = _mix_out(alpha, o4, y, st["w_out"], ln1_g[l][None, :], ln1_b[l][None, :],
                                 st["wr"], st["rb"])
        y = _moe(alpha, h, route, cnt[0, E_LANE:E_LANE + N_EXPERTS], w_gate_up[l], w_down[l],
                 ln2_g[l][None, :], ln2_b[l][None, :])
        rows_p.append((kf[:t], vf[:t], misc[:t, :IDX_DIM], misc[:t, F_LANE:F_LANE + N_GROUP_HEADS]))
        rows_s.append((kf[t:t + n_s], vf[t:t + n_s], misc[t:t + n_s, :IDX_DIM],
                       misc[t:t + n_s, F_LANE:F_LANE + N_GROUP_HEADS]))

    def stack(rows, n, lead):
        tail = {0: (N_HEADS, HEAD_DIM), 1: (N_HEADS, HEAD_DIM), 2: (IDX_DIM,), 3: (N_GROUP_HEADS,)}[n]
        return jnp.stack([r[n].reshape(*lead, *tail) for r in rows])

    outs = [y[:t].reshape(n_seq, seq, d), y[t:t + n_s].reshape(db, dec_seq, d)]
    outs += [stack(rows_p, n, (n_seq, seq)) for n in range(4)]
    outs += [stack(rows_s, n, (db, dec_seq)) for n in range(4)]
    return tuple(outs)
```

```python
import functools
import math

import jax
import jax.numpy as jnp
import numpy as np
from jax import lax
from jax.experimental import pallas as pl
from jax.experimental.pallas import tpu as pltpu

F32 = jnp.float32
BF16 = jnp.bfloat16
I32 = jnp.int32

HEAD_DIM = 64
N_HEADS = 16
N_GROUP_HEADS = 4
DIFF_HALF = 32
IDX_HEADS = 8
IDX_DIM = 64
DSA_TOPK = 256
ROPE_THETA = 10000.0
N_EXPERT_GROUPS = 4
EXPERTS_PER_GROUP = 8
N_EXPERTS = 32
LN_EPS = 1e-5

LANES = 128
VMEM_LIMIT = 56 * 1024 * 1024

TM = 256
TQ = 256
MOE_ROWS = 256
NEG = -1e30
INT_MIN = -2 ** 31

WI_LANE = IDX_DIM
F_LANE = IDX_DIM + IDX_HEADS


def _cparams(sem):
    return pltpu.CompilerParams(dimension_semantics=sem, vmem_limit_bytes=VMEM_LIMIT)


def _dot(a, b):
    return jnp.dot(a, b, preferred_element_type=F32)


def _dot_nt(a, b):
    return lax.dot_general(a, b, (((1,), (1,)), ((), ())), preferred_element_type=F32)


def _split2(x):
    hi = x.astype(BF16)
    lo = (x - hi.astype(F32)).astype(BF16)
    return hi, lo


def _split3(x):
    hi = x.astype(BF16)
    r = x - hi.astype(F32)
    mid = r.astype(BF16)
    lo = (r - mid.astype(F32)).astype(BF16)
    return hi, mid, lo


def _log_sigmoid(z):
    return jnp.minimum(z, 0.0) - jnp.log1p(jnp.exp(-jnp.abs(z)))


def _iota(shape, dim):
    return lax.broadcasted_iota(I32, shape, dim)


def _rope_chunk(y, cos, sin_signed, half):
    lane = _iota(y.shape, 1)
    first = (lane % (2 * half)) < half
    fwd = pltpu.roll(y, LANES - half, 1)
    bwd = pltpu.roll(y, half, 1)
    return y * cos + jnp.where(first, fwd, bwd) * sin_signed


def _proj_kernel(x_ref, w_ref, tab_ref, bf_ref, qs_ref, kb_ref, vb_ref, kf_ref, vf_ref,
                 qib_ref, misc_ref, ki2_ref, vx_ref):
    xb = x_ref[...].astype(BF16)
    cos_d, sin_d = tab_ref[:, 0:128], tab_ref[:, 128:256]
    cos_f, sin_f = tab_ref[:, 256:384], tab_ref[:, 384:512]
    mix = N_HEADS * HEAD_DIM

    def chunk(c):
        return _dot(xb, w_ref[:, c * LANES:(c + 1) * LANES])

    def roped(c, y):
        if c < 2:
            return _rope_chunk(y, cos_d, sin_d, DIFF_HALF // 2)
        if c < 4:
            return _rope_chunk(y, cos_f, sin_f, HEAD_DIM // 2)
        return y

    nch = mix // LANES
    for c in range(nch):
        q = roped(c, chunk(c))
        scale = DIFF_HALF ** -0.5 if c < 2 else HEAD_DIM ** -0.5
        qs_ref[:, c * LANES:(c + 1) * LANES] = (q * scale).astype(BF16)
        k = roped(c, chunk(nch + c))
        kf_ref[:, c * LANES:(c + 1) * LANES] = k
        kb_ref[:, c * LANES:(c + 1) * LANES] = k.astype(BF16)
        v = chunk(2 * nch + c)
        vf_ref[:, c * LANES:(c + 1) * LANES] = v
        vb_ref[:, c * LANES:(c + 1) * LANES] = v.astype(BF16)
        low = _iota(v.shape, 1) < HEAD_DIM
        vx_ref[:, 2 * c * LANES:(2 * c + 1) * LANES] = jnp.where(low, v, 1.0).astype(BF16)
        vx_ref[:, (2 * c + 1) * LANES:(2 * c + 2) * LANES] = jnp.where(
            low, pltpu.roll(v, HEAD_DIM, 1), 1.0).astype(BF16)
    for c in range(IDX_HEADS * IDX_DIM // LANES):
        qi = _rope_chunk(chunk(3 * nch + c), cos_f, sin_f, IDX_DIM // 2)
        qib_ref[:, c * LANES:(c + 1) * LANES] = qi.astype(BF16)
    raw = chunk(3 * nch + IDX_HEADS * IDX_DIM // LANES)
    ki = _rope_chunk(raw, cos_f, sin_f, IDX_DIM // 2)
    lane = _iota(raw.shape, 1)
    wi_scale = IDX_DIM ** -0.5 * IDX_HEADS ** -0.5
    logf = _log_sigmoid(raw + bf_ref[...])
    misc = jnp.where(lane < IDX_DIM, ki,
                     jnp.where(lane < F_LANE, raw * wi_scale,
                               jnp.where(lane < F_LANE + N_GROUP_HEADS, logf, 0.0)))
    misc_ref[...] = misc
    ki2_ref[...] = jnp.where(lane < IDX_DIM, ki, pltpu.roll(ki, IDX_DIM, 1)).astype(BF16)


def _project(x, w_pad, tab, bf_row, n_prompt_tiles, tiles_per_seq):
    tp, d = x.shape
    mix = N_HEADS * HEAD_DIM
    nt = tp // TM

    def tab_map(t):
        return (jnp.where(t < n_prompt_tiles, t % tiles_per_seq, tiles_per_seq + t - n_prompt_tiles), 0)

    row = lambda t: (t, 0)
    fixed = lambda t: (0, 0)
    out_shapes = (
        jax.ShapeDtypeStruct((tp, mix), BF16), jax.ShapeDtypeStruct((tp, mix), BF16),
        jax.ShapeDtypeStruct((tp, mix), BF16), jax.ShapeDtypeStruct((tp, mix), F32),
        jax.ShapeDtypeStruct((tp, mix), F32), jax.ShapeDtypeStruct((tp, IDX_HEADS * IDX_DIM), BF16),
        jax.ShapeDtypeStruct((tp, LANES), F32), jax.ShapeDtypeStruct((tp, LANES), BF16),
        jax.ShapeDtypeStruct((tp, 2 * mix), BF16))
    return pl.pallas_call(
        _proj_kernel,
        grid=(nt,),
        in_specs=[pl.BlockSpec((TM, d), row), pl.BlockSpec(w_pad.shape, fixed),
                  pl.BlockSpec((TM, 512), tab_map), pl.BlockSpec((1, LANES), fixed)],
        out_specs=(pl.BlockSpec((TM, mix), row), pl.BlockSpec((TM, mix), row), pl.BlockSpec((TM, mix), row),
                   pl.BlockSpec((TM, mix), row), pl.BlockSpec((TM, mix), row),
                   pl.BlockSpec((TM, IDX_HEADS * IDX_DIM), row), pl.BlockSpec((TM, LANES), row),
                   pl.BlockSpec((TM, LANES), row), pl.BlockSpec((TM, 2 * mix), row)),
        out_shape=out_shapes,
        compiler_params=_cparams(("parallel",)),
        name="proj",
    )(x, w_pad, tab, bf_row)


def _rope_tables(pos):
    def pattern(half):
        lane = jnp.arange(LANES)
        inv = ROPE_THETA ** (-(lane % half).astype(F32) / half)
        ang = pos.astype(F32)[:, None] * inv[None, :]
        sign = jnp.where((lane % (2 * half)) < half, -1.0, 1.0)
        return jnp.cos(ang), jnp.sin(ang) * sign[None, :]
    cd, sd = pattern(DIFF_HALF // 2)
    cf, sf = pattern(HEAD_DIM // 2)
    return jnp.concatenate([cd, sd, cf, sf], axis=1)


def _fcum_kernel(misc_ref, fcol_ref, ft_ref):
    s = misc_ref.shape[0]
    r = _iota((TQ, TQ), 0)
    c = _iota((TQ, TQ), 1)
    tri = (r >= c).astype(BF16)
    sel = (_iota((8, LANES), 1) == _iota((8, LANES), 0) + F_LANE).astype(BF16)
    carry = jnp.zeros((1, LANES), F32)
    for b in range(s // TQ):
        seg = misc_ref[b * TQ:(b + 1) * TQ, :]
        cs = carry
        for piece in _split3(seg):
            cs = cs + _dot(tri, piece)
        fcol_ref[b * TQ:(b + 1) * TQ, :] = cs
        tr = jnp.zeros((8, TQ), F32)
        for piece in _split3(cs):
            tr = tr + _dot_nt(sel, piece)
        ft_ref[b] = tr
        carry = cs[TQ - 1:TQ, :]


def _fcum(misc, n_seq, seq):
    nkb = seq // TQ
    return pl.pallas_call(
        _fcum_kernel,
        grid=(n_seq,),
        in_specs=[pl.BlockSpec((seq, LANES), lambda b: (b, 0))],
        out_specs=(pl.BlockSpec((seq, LANES), lambda b: (b, 0)),
                   pl.BlockSpec((nkb, 8, TQ), lambda b: (b, 0, 0))),
        out_shape=(jax.ShapeDtypeStruct((n_seq * seq, LANES), F32),
                   jax.ShapeDtypeStruct((n_seq * nkb, 8, TQ), F32)),
        compiler_params=_cparams(("parallel",)),
        name="fcum",
    )(misc)


def _half_mask(width, offset, size):
    lane = _iota((1, LANES), 1)
    return (lane >= offset) & (lane < offset + size)


def _masked(q, offset, size):
    return jnp.where(_half_mask(LANES, offset, size), q, jnp.zeros_like(q))


def _kblock(ref, j, width, col):
    return ref[pl.ds(pl.multiple_of(j * TQ, TQ), TQ), col * width:(col + 1) * width]


def _head_q(q_ref, hh, offset=0, size=HEAD_DIM):
    qp = q_ref[:, (hh // 2) * LANES:(hh // 2 + 1) * LANES]
    return _masked(qp, (hh % 2) * HEAD_DIM + offset, size)


def _diag_mask(strict=False):
    row = _iota((TQ, TQ), 0)
    col = _iota((TQ, TQ), 1)
    return (col < row) if strict else (col <= row)


def _softmax_reset(m_ref, acc_ref):
    m_ref[...] = jnp.full(m_ref.shape, NEG, F32)
    acc_ref[...] = jnp.zeros_like(acc_ref)


def _wide(x):
    return jnp.concatenate([x] * (TQ // LANES), axis=1)


def _row_bcast(col):
    return jnp.broadcast_to(col, (TQ, LANES))


def _softmax_block(n, s, vx, m_ref, acc_ref):
    m_old = m_ref[n]
    m_new = jnp.maximum(m_old, _row_bcast(jnp.max(s, axis=1, keepdims=True)))
    alpha = jnp.exp(m_old - m_new)
    p = jnp.exp(s - _wide(m_new))
    acc_ref[n] = alpha * acc_ref[n] + _dot(p.astype(BF16), vx)
    m_ref[n] = m_new


def _softmax_result(n, acc_ref):
    acc = acc_ref[n]
    return acc / pltpu.roll(acc, HEAD_DIM, 1)


def _store_heads(o_ref, outs):
    lane = _iota((TQ, LANES), 1)
    for pair in range(2):
        o = jnp.where(lane < HEAD_DIM, outs[2 * pair], pltpu.roll(outs[2 * pair + 1], HEAD_DIM, 1))
        o_ref[:, pair * LANES:(pair + 1) * LANES] = o.astype(o_ref.dtype)


def _attn_diff_kernel(lam_init, q_ref, k_ref, vx_ref, lamv_ref, subg_ref, o_ref, m_ref, acc_ref):
    i = pl.program_id(1)
    lv = lamv_ref[...]
    lam = (jnp.exp(jnp.sum(lv[0:1] * lv[1:2], axis=1, keepdims=True))
           - jnp.exp(jnp.sum(lv[2:3] * lv[3:4], axis=1, keepdims=True))) + lam_init
    _softmax_reset(m_ref, acc_ref)
    qs = [_head_q(q_ref, hh, part * DIFF_HALF, DIFF_HALF) for hh in range(N_GROUP_HEADS) for part in range(2)]

    def block(j, mask):
        for hh in range(N_GROUP_HEADS):
            kp = _kblock(k_ref, j, LANES, hh // 2)
            vx = _kblock(vx_ref, j, LANES, hh)
            for part in range(2):
                s = _dot_nt(qs[2 * hh + part], kp)
                if mask is not None:
                    s = jnp.where(mask, s, NEG)
                _softmax_block(2 * hh + part, s, vx, m_ref, acc_ref)

    def body(j, carry):
        block(j, None)
        return carry
    lax.fori_loop(0, i, body, 0)
    block(i, _diag_mask())

    outs = []
    lane = _iota((TQ, LANES), 1)
    for hh in range(N_GROUP_HEADS):
        oa = _softmax_result(2 * hh, acc_ref) - lam * _softmax_result(2 * hh + 1, acc_ref)
        ms = jnp.sum(jnp.where(lane < HEAD_DIM, oa * oa, 0.0), axis=1, keepdims=True) / HEAD_DIM
        outs.append(oa * lax.rsqrt(ms + LN_EPS) * subg_ref[...] * (1.0 - lam_init))
    _store_heads(o_ref, outs)


def _attn_fox_kernel(q_ref, k_ref, vx_ref, fcol_ref, ft_ref, o_ref, m_ref, acc_ref):
    i = pl.program_id(1)
    _softmax_reset(m_ref, acc_ref)
    qs = [_head_q(q_ref, hh) for hh in range(N_GROUP_HEADS)]
    fqs = [_row_bcast(fcol_ref[:, F_LANE + g:F_LANE + g + 1]) for g in range(N_GROUP_HEADS)]

    def block(j, mask):
        ft = ft_ref[j]
        for hh in range(N_GROUP_HEADS):
            s = _dot_nt(qs[hh], _kblock(k_ref, j, LANES, hh // 2)) + (_wide(fqs[hh]) - ft[hh:hh + 1, :])
            if mask is not None:
                s = jnp.where(mask, s, NEG)
            _softmax_block(hh, s, _kblock(vx_ref, j, LANES, hh), m_ref, acc_ref)

    def body(j, carry):
        block(j, None)
        return carry
    lax.fori_loop(0, i, body, 0)
    block(i, _diag_mask())
    _store_heads(o_ref, [_softmax_result(hh, acc_ref) for hh in range(N_GROUP_HEADS)])


def _attn_sb_kernel(q_ref, k_ref, vx_ref, o_ref, c_ref, acc_ref):
    i = pl.program_id(1)
    suffix = (_iota((TQ, TQ), 0) >= _iota((TQ, TQ), 1)).astype(BF16)
    c_ref[...] = jnp.zeros_like(c_ref)
    acc_ref[...] = jnp.zeros_like(acc_ref)
    qs = [_head_q(q_ref, hh) for hh in range(N_GROUP_HEADS)]

    def block(j, mask):
        for hh in range(N_GROUP_HEADS):
            z = _dot_nt(qs[hh], _kblock(k_ref, j, LANES, hh // 2))
            ls = jnp.minimum(z, 0.0) - jnp.log(1.0 + jnp.exp(-jnp.abs(z)))
            l1m = ls - z
            if mask is not None:
                l1m = jnp.where(mask, l1m, 0.0)
            hi, lo = _split2(l1m)
            incl = _dot(hi, suffix) + _dot(lo, suffix)
            carry = c_ref[hh]
            w = jnp.exp(ls + (incl - l1m + _wide(carry)))
            if mask is not None:
                w = jnp.where(mask, w, 0.0)
            acc_ref[hh] = acc_ref[hh] + _dot(w.astype(BF16), _kblock(vx_ref, j, LANES, hh))
            c_ref[hh] = carry + _row_bcast(incl[:, 0:1])

    block(i, _diag_mask(strict=True))

    def body(jj, carry):
        block(i - 1 - jj, None)
        return carry
    lax.fori_loop(0, i, body, 0)
    _store_heads(o_ref, [acc_ref[hh] for hh in range(N_GROUP_HEADS)])


def _sort_key(x):
    bits = pltpu.bitcast(x, I32)
    return bits ^ ((bits >> 31) & jnp.int32(0x7FFFFFFF))


def _topk_threshold(count_ge, n_sel, shape):
    def body(b, tau):
        inc = lax.shift_left(jnp.int32(1), jnp.int32(31) - b)
        cand = tau + inc
        return jnp.where(count_ge(cand) >= n_sel, cand, tau)
    return lax.fori_loop(0, 32, body, jnp.full(shape, INT_MIN, I32))


def _attn_dsa_kernel(n_sel, q_ref, k_ref, vx_ref, qi_ref, ki2_ref, misc_ref, o_ref, key_ref, sel_ref,
                     m_ref, acc_ref):
    i = pl.program_id(1)
    nb = i + 1

    qis = [_masked(qi_ref[:, (ih // 2) * LANES:(ih // 2 + 1) * LANES], (ih % 2) * IDX_DIM, IDX_DIM)
           for ih in range(IDX_HEADS)]
    ws = [misc_ref[:, WI_LANE + ih:WI_LANE + ih + 1] for ih in range(IDX_HEADS)]

    def scores(j):
        kip = _kblock(ki2_ref, j, LANES, 0)
        isc = jnp.zeros((TQ, TQ), F32)
        for ih in range(IDX_HEADS):
            isc = isc + jnp.maximum(_dot_nt(qis[ih], kip), 0.0) * ws[ih]
        return _sort_key(isc)

    def score_block(j, carry):
        key_ref[j] = scores(j)
        return carry
    lax.fori_loop(0, i, score_block, 0)
    key_ref[i] = jnp.where(_diag_mask(), scores(i), INT_MIN)

    def count(pred):
        def body(j, acc):
            key = key_ref[j]
            for c in range(TQ // LANES):
                acc = acc + jnp.where(pred(key[:, c * LANES:(c + 1) * LANES]), 1.0, 0.0)
            return acc
        acc = lax.fori_loop(0, nb, body, jnp.zeros((TQ, LANES), F32))
        return _row_bcast(jnp.sum(acc, axis=1, keepdims=True))

    tau = _topk_threshold(lambda cand: count(lambda k: k >= cand), float(n_sel), (TQ, LANES))
    need = _wide(float(n_sel) - count(lambda k: k > tau))
    tau = _wide(tau)

    before = (_iota((TQ, TQ), 0) < _iota((TQ, TQ), 1)).astype(BF16)
    ones = jnp.ones((TQ, LANES), BF16)

    def select_block(j, seen):
        key = key_ref[j]
        eq = (key == tau) & (key != INT_MIN)
        eqb = jnp.where(eq, 1.0, 0.0).astype(BF16)
        prefix = _dot(eqb, before) + _wide(seen)
        sel = (key > tau) | (eq & (prefix < need))
        sel_ref[j] = jnp.where(sel, 0.0, NEG)
        return seen + _dot(eqb, ones)
    lax.fori_loop(0, nb, select_block, jnp.zeros((TQ, LANES), F32))

    _softmax_reset(m_ref, acc_ref)
    qs = [_head_q(q_ref, hh) for hh in range(N_GROUP_HEADS)]

    def body(j, carry):
        bias = sel_ref[j]
        for hh in range(N_GROUP_HEADS):
            s = _dot_nt(qs[hh], _kblock(k_ref, j, LANES, hh // 2)) + bias
            _softmax_block(hh, s, _kblock(vx_ref, j, LANES, hh), m_ref, acc_ref)
        return carry
    lax.fori_loop(0, nb, body, 0)
    _store_heads(o_ref, [_softmax_result(hh, acc_ref) for hh in range(N_GROUP_HEADS)])


def _prompt_attention(lam_init, qs, kb, vx, qib, ki2b, misc, fcol, ft, lamv, subg2, n_seq, seq):
    nq = seq // TQ
    nkb = seq // TQ
    t = n_seq * seq
    grp = N_GROUP_HEADS * HEAD_DIM

    def qspec(g):
        return pl.BlockSpec((TQ, grp), lambda b, i: (b * nq + i, g))

    def kspec(g):
        return pl.BlockSpec((seq, grp), lambda b, i: (b, g))

    def vspec(g):
        return pl.BlockSpec((seq, 2 * grp), lambda b, i: (b, g))

    ospec = pl.BlockSpec((TQ, grp), lambda b, i: (b * nq + i, 0))
    oshape = jax.ShapeDtypeStruct((t, grp), BF16)
    fixed = lambda b, i: (0, 0)
    rowblk = lambda b, i: (b * nq + i, 0)
    params = _cparams(("parallel", "arbitrary"))

    def state(n):
        return [pltpu.VMEM((n, TQ, LANES), F32), pltpu.VMEM((n, TQ, LANES), F32)]

    o_a = pl.pallas_call(
        functools.partial(_attn_diff_kernel, lam_init), grid=(n_seq, nq),
        in_specs=[qspec(0), kspec(0), vspec(0), pl.BlockSpec((8, LANES), fixed),
                  pl.BlockSpec((1, LANES), fixed)],
        out_specs=ospec, out_shape=oshape, scratch_shapes=state(2 * N_GROUP_HEADS),
        compiler_params=params, name="attn_diff",
    )(qs, kb, vx, lamv, subg2)
    n_sel = min(DSA_TOPK, seq // 4)
    o_b = pl.pallas_call(
        functools.partial(_attn_dsa_kernel, n_sel), grid=(n_seq, nq),
        in_specs=[qspec(1), kspec(1), vspec(1),
                  pl.BlockSpec((TQ, IDX_HEADS * IDX_DIM), rowblk),
                  pl.BlockSpec((seq, LANES), lambda b, i: (b, 0)),
                  pl.BlockSpec((TQ, LANES), rowblk)],
        out_specs=ospec, out_shape=oshape,
        scratch_shapes=[pltpu.VMEM((nkb, TQ, TQ), I32), pltpu.VMEM((nkb, TQ, TQ), F32)] + state(N_GROUP_HEADS),
        compiler_params=params, name="attn_dsa",
    )(qs, kb, vx, qib, ki2b, misc)
    o_c = pl.pallas_call(
        _attn_sb_kernel, grid=(n_seq, nq),
        in_specs=[qspec(2), kspec(2), vspec(2)],
        out_specs=ospec, out_shape=oshape, scratch_shapes=state(N_GROUP_HEADS),
        compiler_params=params, name="attn_sb",
    )(qs, kb, vx)
    o_d = pl.pallas_call(
        _attn_fox_kernel, grid=(n_seq, nq),
        in_specs=[qspec(3), kspec(3), vspec(3), pl.BlockSpec((TQ, LANES), rowblk),
                  pl.BlockSpec((nkb, 8, TQ), lambda b, i: (b, 0, 0))],
        out_specs=ospec, out_shape=oshape, scratch_shapes=state(N_GROUP_HEADS),
        compiler_params=params, name="attn_fox",
    )(qs, kb, vx, fcol, ft)
    return o_a, o_b, o_c, o_d


PAGE = 128
CP = 4
CK = CP * PAGE
T8 = 8
SLOT_A1, SLOT_A2, SLOT_B, SLOT_C, SLOT_D = range(5)
N_QROWS = 5 * N_GROUP_HEADS * T8


def _dec_index_kernel(n_sel, dec_seq, pt_ref, *refs):
    ki_refs = refs[:CP]
    lf_refs = refs[CP:2 * CP]
    qi_ref, w_ref, kin_ref, selp_ref, seln_ref, dsuf_ref, key_ref, carry_ref = refs[2 * CP:]
    c = pl.program_id(1)
    nch = pl.num_programs(1)

    def scores(s):
        s = jnp.maximum(s, 0.0) * w_ref[0][:, 0:1]
        isc = jnp.zeros((T8, s.shape[1]), F32)
        for ih in range(IDX_HEADS):
            isc = isc + s[ih * T8:(ih + 1) * T8, :]
        return isc

    kit = jnp.concatenate([r[0, 0] for r in ki_refs], axis=1).astype(BF16)
    key_ref[nch - 1 - c] = _sort_key(scores(_dot(qi_ref[0], kit)))

    @pl.when(c == 0)
    def _():
        carry_ref[...] = jnp.zeros_like(carry_ref)
    lf = jnp.concatenate([r[0, 0] for r in lf_refs], axis=1)
    after = (_iota((CK, CK), 0) > _iota((CK, CK), 1)).astype(BF16)
    carry = carry_ref[:, 0:1]
    suf = carry
    for piece in _split3(lf):
        suf = suf + _dot(piece, after)
    dsuf_ref[0, 0] = suf
    carry_ref[...] = jnp.broadcast_to(carry + jnp.sum(lf, axis=1, keepdims=True), carry_ref.shape)

    @pl.when(c == nch - 1)
    def _():
        t8 = _iota((T8, PAGE), 0)
        j = _iota((T8, PAGE), 1)
        keyn = jnp.where((j <= t8) & (j < dec_seq),
                         _sort_key(scores(_dot_nt(qi_ref[0], kin_ref[0].astype(BF16)))), INT_MIN)
        n_blk = key_ref.shape[0]

        def count(pred):
            def body(b, acc):
                return acc + jnp.where(pred(key_ref[b]), 1.0, 0.0)
            acc = lax.fori_loop(0, n_blk, body, jnp.zeros((T8, CK), F32))
            return (jnp.sum(acc, axis=1, keepdims=True)
                    + jnp.sum(jnp.where(pred(keyn), 1.0, 0.0), axis=1, keepdims=True))

        tau = _topk_threshold(lambda cand: count(lambda k: k >= cand), float(n_sel), (T8, 1))
        need = float(n_sel) - count(lambda k: k > tau)
        before = (_iota((CK, CK), 0) < _iota((CK, CK), 1)).astype(BF16)

        def select(key, seen, tri):
            eq = (key == tau) & (key != INT_MIN)
            eqf = jnp.where(eq, 1.0, 0.0)
            prefix = _dot(eqf.astype(BF16), tri) + seen
            sel = (key > tau) | (eq & (prefix < need))
            return jnp.where(sel, 0.0, NEG), seen + jnp.sum(eqf, axis=1, keepdims=True)

        def body(b, seen):
            bias, seen = select(key_ref[b], seen, before)
            selp_ref[0, b] = bias
            return seen
        seen = lax.fori_loop(0, n_blk, body, jnp.zeros((T8, 1), F32))
        bias, _ = select(keyn, seen, before[:PAGE, :PAGE])
        seln_ref[0] = bias


def _dec_index(l, n_sel, dec_seq, pt_flat, idx_t, logf_t, qim, wv, ki_new, n_pages):
    db = qim.shape[0]
    nch = n_pages // CP

    def page_spec(u, rows):
        return pl.BlockSpec((1, 1, rows, PAGE),
                            lambda b, c, pt: (l, pt[b * n_pages + (nch - 1 - c) * CP + u], 0, 0))

    per_b = lambda b, c, pt: (b, 0, 0)
    chunk = lambda b, c, pt: (b, nch - 1 - c, 0, 0)
    return pl.pallas_call(
        functools.partial(_dec_index_kernel, n_sel, dec_seq),
        grid_spec=pltpu.PrefetchScalarGridSpec(
            num_scalar_prefetch=1,
            grid=(db, nch),
            in_specs=[page_spec(u, IDX_DIM) for u in range(CP)] + [page_spec(u, 8) for u in range(CP)] + [
                pl.BlockSpec((1, IDX_HEADS * T8, IDX_DIM), per_b),
                pl.BlockSpec((1, IDX_HEADS * T8, LANES), per_b),
                pl.BlockSpec((1, PAGE, IDX_DIM), per_b)],
            out_specs=(pl.BlockSpec((1, nch, T8, CK), lambda b, c, pt: (b, 0, 0, 0)),
                       pl.BlockSpec((1, T8, PAGE), per_b),
                       pl.BlockSpec((1, 1, 8, CK), chunk)),
            scratch_shapes=[pltpu.VMEM((nch, T8, CK), I32), pltpu.VMEM((8, LANES), F32)]),
        out_shape=(jax.ShapeDtypeStruct((db, nch, T8, CK), F32), jax.ShapeDtypeStruct((db, T8, PAGE), F32),
                   jax.ShapeDtypeStruct((db, nch, 8, CK), F32)),
        compiler_params=_cparams(("parallel", "arbitrary")),
        name="dec_index",
    )(pt_flat, *([idx_t] * CP), *([logf_t] * CP), qim, wv, ki_new)


def _slot_rows(slot, hh):
    return (slot * N_GROUP_HEADS + hh) * T8


def _dec_attn_kernel(lam_init, dec_seq, pt_ref, *refs):
    k_refs = refs[:CP]
    v_refs = refs[CP:2 * CP]
    (q_ref, kn_ref, vn_ref, selp_ref, seln_ref, dsuf_ref, miscn_ref, lamv_ref, subg_ref,
     o_ref, m_ref, l_ref, cc_ref, acc_ref) = refs[2 * CP:]
    c = pl.program_id(1)
    nch = pl.num_programs(1)
    rows = _iota((N_QROWS, 1), 0)
    is_c = (rows >= _slot_rows(SLOT_C, 0)) & (rows < _slot_rows(SLOT_D, 0))
    c0, c1 = _slot_rows(SLOT_C, 0), _slot_rows(SLOT_D, 0)
    b0 = _slot_rows(SLOT_B, 0)
    q = q_ref[0]

    lfn = miscn_ref[0]
    t8 = _iota((T8, LANES), 0)
    cn = jnp.zeros((T8, LANES), F32)
    for j in range(dec_seq):
        cn = cn + jnp.where(t8 >= j, lfn[j:j + 1, :], 0.0)

    def update(s, bias, ok_c, pv, suffix):
        m_old, l_old, carry = m_ref[:, 0:1], l_ref[:, 0:1], cc_ref[:, 0:1]
        sm = s + bias
        m_new = jnp.maximum(m_old, jnp.max(sm, axis=1, keepdims=True))
        alpha = jnp.exp(m_old - m_new)
        p = jnp.exp(sm - m_new)
        l_new = alpha * l_old + jnp.sum(p, axis=1, keepdims=True)
        z = s[c0:c1]
        ls = _log_sigmoid(z)
        l1m = jnp.where(ok_c, ls - z, 0.0)
        hi, lo = _split2(l1m)
        incl = _dot(hi, suffix) + _dot(lo, suffix)
        tail = incl - l1m + carry
        w = jnp.where(ok_c, jnp.exp(ls + tail), 0.0)
        pw = jnp.concatenate([p[:c0], w, p[c1:]], axis=0).astype(BF16)
        acc_ref[...] = jnp.where(is_c, 1.0, alpha) * acc_ref[...] + pv(pw)
        m_ref[...] = jnp.broadcast_to(m_new, m_ref.shape)
        l_ref[...] = jnp.broadcast_to(l_new, l_ref.shape)
        cc_ref[...] = jnp.broadcast_to(carry + incl[:, 0:1], cc_ref.shape)

    @pl.when(c == 0)
    def _():
        m_ref[...] = jnp.full(m_ref.shape, NEG, F32)
        l_ref[...] = jnp.zeros_like(l_ref)
        cc_ref[...] = jnp.zeros_like(cc_ref)
        acc_ref[...] = jnp.zeros_like(acc_ref)
        kn, vn = kn_ref[0], vn_ref[0]
        s = _dot_nt(q, kn)
        jn = _iota((N_QROWS, PAGE), 1)
        tn = _iota((N_QROWS, PAGE), 0) % T8
        live = jn < dec_seq
        sel_g = (_iota((8, LANES), 1) == _iota((8, LANES), 0) + F_LANE).astype(BF16)
        cn_pad = jnp.concatenate([cn, jnp.zeros((PAGE - T8, LANES), F32)], axis=0)
        cnt = jnp.zeros((8, PAGE), F32)
        for piece in _split3(cn_pad):
            cnt = cnt + _dot_nt(sel_g, piece)
        blocks = [jnp.zeros((b0, PAGE), F32)]
        blocks += [seln_ref[0]] * N_GROUP_HEADS
        blocks += [jnp.zeros((c1 - c0, PAGE), F32)]
        for g in range(N_GROUP_HEADS):
            blocks.append(cn[:, F_LANE + g:F_LANE + g + 1] - cnt[g:g + 1, :])
        bias = jnp.where(live & (jn <= tn), jnp.concatenate(blocks, axis=0), NEG)
        ok_c = (live & (jn < tn))[c0:c1]
        suffix = (_iota((PAGE, PAGE), 0) >= _iota((PAGE, PAGE), 1)).astype(BF16)
        update(s, bias, ok_c, lambda pw: _dot(pw, vn), suffix)

    kt = jnp.concatenate([r[0, 0].astype(BF16) for r in k_refs], axis=1)
    vt = jnp.concatenate([r[0, 0].astype(BF16) for r in v_refs], axis=1)
    s = _dot(q, kt)
    blocks = [jnp.zeros((b0, CK), F32)]
    blocks += [selp_ref[0, 0]] * N_GROUP_HEADS
    blocks += [jnp.zeros((c1 - c0, CK), F32)]
    dsuf = dsuf_ref[0, 0]
    for g in range(N_GROUP_HEADS):
        blocks.append(cn[:, F_LANE + g:F_LANE + g + 1] + dsuf[g:g + 1, :])
    bias = jnp.concatenate(blocks, axis=0)
    suffix = (_iota((CK, CK), 0) >= _iota((CK, CK), 1)).astype(BF16)
    update(s, bias, jnp.full((c1 - c0, CK), True), lambda pw: _dot_nt(pw, vt), suffix)

    @pl.when(c == nch - 1)
    def _():
        lv = lamv_ref[...]
        lam = (jnp.exp(jnp.sum(lv[0:1] * lv[1:2], axis=1, keepdims=True))
               - jnp.exp(jnp.sum(lv[2:3] * lv[3:4], axis=1, keepdims=True))) + lam_init
        o_all = acc_ref[...] / jnp.where(is_c, 1.0, l_ref[:, 0:1])
        lane = _iota((T8, N_HEADS * HEAD_DIM), 1)
        out = jnp.zeros((T8, N_HEADS * HEAD_DIM), F32)
        for hh in range(N_GROUP_HEADS):
            r1, r2 = _slot_rows(SLOT_A1, hh), _slot_rows(SLOT_A2, hh)
            oa = o_all[r1:r1 + T8] - lam * o_all[r2:r2 + T8]
            hm = (lane >= hh * HEAD_DIM) & (lane < (hh + 1) * HEAD_DIM)
            ms = jnp.sum(jnp.where(hm, oa * oa, 0.0), axis=1, keepdims=True) / HEAD_DIM
            oa = oa * lax.rsqrt(ms + LN_EPS) * subg_ref[...] * (1.0 - lam_init)
            out = jnp.where(hm, oa, out)
            for grp, slot in ((1, SLOT_B), (2, SLOT_C), (3, SLOT_D)):
                r = _slot_rows(slot, hh)
                head = grp * N_GROUP_HEADS + hh
                hm = (lane >= head * HEAD_DIM) & (lane < (head + 1) * HEAD_DIM)
                out = jnp.where(hm, o_all[r:r + T8], out)
        o_ref[0] = out.astype(o_ref.dtype)


def _dec_attention(l, lam_init, dec_seq, pt_flat, cache_k, cache_v, qall, k_new, v_new, selp, seln, dsuf,
                   misc_new, lamv, subg16, n_pages):
    db = qall.shape[0]
    nch = n_pages // CP
    mix = N_HEADS * HEAD_DIM

    def page_spec(u):
        return pl.BlockSpec((1, 1, mix, PAGE),
                            lambda b, c, pt: (l, pt[b * n_pages + (nch - 1 - c) * CP + u], 0, 0))

    per_b = lambda b, c, pt: (b, 0, 0)
    chunk = lambda b, c, pt: (b, nch - 1 - c, 0, 0)
    fixed = lambda b, c, pt: (0, 0)
    return pl.pallas_call(
        functools.partial(_dec_attn_kernel, lam_init, dec_seq),
        grid_spec=pltpu.PrefetchScalarGridSpec(
            num_scalar_prefetch=1,
            grid=(db, nch),
            in_specs=[page_spec(u) for u in range(CP)] * 2 + [
                pl.BlockSpec((1, N_QROWS, mix), per_b),
                pl.BlockSpec((1, PAGE, mix), per_b), pl.BlockSpec((1, PAGE, mix), per_b),
                pl.BlockSpec((1, 1, T8, CK), chunk), pl.BlockSpec((1, T8, PAGE), per_b),
                pl.BlockSpec((1, 1, 8, CK), chunk), pl.BlockSpec((1, T8, LANES), per_b),
                pl.BlockSpec((8, LANES), fixed), pl.BlockSpec((1, mix), fixed)],
            out_specs=pl.BlockSpec((1, T8, mix), per_b),
            scratch_shapes=[pltpu.VMEM((N_QROWS, LANES), F32), pltpu.VMEM((N_QROWS, LANES), F32),
                            pltpu.VMEM((_slot_rows(SLOT_D, 0) - _slot_rows(SLOT_C, 0), LANES), F32),
                            pltpu.VMEM((N_QROWS, mix), F32)]),
        out_shape=jax.ShapeDtypeStruct((db, T8, mix), BF16),
        compiler_params=_cparams(("parallel", "arbitrary")),
        name="dec_attn",
    )(pt_flat, *([cache_k] * CP), *([cache_v] * CP), qall, k_new, v_new, selp, seln, dsuf, misc_new, lamv, subg16)


def _slot_lane_masks():
    mix = N_HEADS * HEAD_DIM
    m = np.zeros((5 * N_GROUP_HEADS, mix), np.float32)
    for hh in range(N_GROUP_HEADS):
        m[SLOT_A1 * 4 + hh, hh * HEAD_DIM:hh * HEAD_DIM + DIFF_HALF] = 1
        m[SLOT_A2 * 4 + hh, hh * HEAD_DIM + DIFF_HALF:(hh + 1) * HEAD_DIM] = 1
        for grp, slot in ((1, SLOT_B), (2, SLOT_C), (3, SLOT_D)):
            head = grp * N_GROUP_HEADS + hh
            m[slot * 4 + hh, head * HEAD_DIM:(head + 1) * HEAD_DIM] = 1
    return m


def _sample_attention(l, lam_init, p, rows, lamv, subg2, db, dec_seq):
    qs, kb, vb, qib, misc = rows
    n_s = db * dec_seq
    mix = N_HEADS * HEAD_DIM
    page_table = p["page_table"]
    n_pages = page_table.shape[1]
    pt_flat = page_table.reshape(-1).astype(I32)
    n_keys = n_pages * PAGE + dec_seq
    n_sel = min(DSA_TOPK, n_keys // 4)

    def per_seq(a, pad_to):
        a = a[:n_s].reshape(db, dec_seq, a.shape[-1])
        return jnp.pad(a, ((0, 0), (0, pad_to - dec_seq), (0, 0)))

    misc8 = per_seq(misc, T8)
    qi8 = per_seq(qib, T8).reshape(db, T8, IDX_HEADS, IDX_DIM)
    qim = jnp.swapaxes(qi8, 1, 2).reshape(db, IDX_HEADS * T8, IDX_DIM)
    wv = jnp.swapaxes(misc8[:, :, WI_LANE:WI_LANE + IDX_HEADS], 1, 2).reshape(db, IDX_HEADS * T8, 1)
    wv = jnp.broadcast_to(wv, (db, IDX_HEADS * T8, LANES))
    ki_new = per_seq(misc, PAGE)[:, :, :IDX_DIM]
    selp, seln, dsuf = _dec_index(l, n_sel, dec_seq, pt_flat, p["cache_idx"], p["cache_logf"], qim, wv,
                                  ki_new, n_pages)

    masks = jnp.asarray(_slot_lane_masks(), BF16)
    q8 = per_seq(qs, T8)
    qall = (q8[:, None, :, :] * masks[None, :, None, :]).reshape(db, N_QROWS, mix)
    subg16 = jnp.tile(subg2, (1, mix // LANES))
    o8 = _dec_attention(l, lam_init, dec_seq, pt_flat, p["cache_k"], p["cache_v"], qall,
                        per_seq(kb, PAGE), per_seq(vb, PAGE), selp, seln, dsuf, misc8, lamv, subg16, n_pages)
    return o8[:, :dec_seq].reshape(n_s, mix)


E_LANE = N_EXPERT_GROUPS
BIG_LANE = 4096


def _layer_norm(x, g, b):
    mu = jnp.mean(x, axis=1, keepdims=True)
    xc = x - mu
    var = jnp.mean(xc * xc, axis=1, keepdims=True)
    return xc * lax.rsqrt(var + LN_EPS) * g + b


def _first_lane(mask, lane):
    return jnp.min(jnp.where(mask, lane, BIG_LANE), axis=1, keepdims=True)


def _mix_out_kernel(alpha, oa_ref, ob_ref, oc_ref, od_ref, x_ref, wo_ref, g_ref, b_ref, wr_ref, rb_ref,
                    h_ref, route_ref, cnt_ref, carry_ref):
    t = pl.program_id(0)
    grp_w = N_GROUP_HEADS * HEAD_DIM
    acc = x_ref[...] * alpha
    for n, o_ref in enumerate((oa_ref, ob_ref, oc_ref, od_ref)):
        acc = acc + _dot(o_ref[...], wo_ref[n * grp_w:(n + 1) * grp_w, :])
    h = _layer_norm(acc, g_ref[...], b_ref[...])
    h_ref[...] = h

    h_hi, h_lo = _split2(h)
    lg = (_dot(h_hi, wr_ref[0]) + _dot(h_lo, wr_ref[0]) + _dot(h_hi, wr_ref[1])) + rb_ref[...]
    lane = _iota(lg.shape, 1)
    is_g = lane < E_LANE
    is_e = (lane >= E_LANE) & (lane < E_LANE + N_EXPERTS)
    gmax = jnp.max(jnp.where(is_g, lg, NEG), axis=1, keepdims=True)
    grp = _first_lane(is_g & (lg == gmax), lane)
    pg_top = 1.0 / jnp.sum(jnp.where(is_g, jnp.exp(lg - gmax), 0.0), axis=1, keepdims=True)
    in_grp = is_e & (((lane - E_LANE) // EXPERTS_PER_GROUP) == grp)
    emax = jnp.max(jnp.where(in_grp, lg, NEG), axis=1, keepdims=True)
    ex = jnp.where(in_grp, jnp.exp(lg - emax), 0.0)
    pe = ex / jnp.sum(ex, axis=1, keepdims=True)
    p0 = jnp.max(jnp.where(in_grp, pe, -1.0), axis=1, keepdims=True)
    i0 = _first_lane(in_grp & (pe == p0), lane)
    rest = in_grp & (lane != i0)
    p1 = jnp.max(jnp.where(rest, pe, -1.0), axis=1, keepdims=True)
    i1 = _first_lane(rest & (pe == p1), lane)
    g0 = pg_top * p0 / (p0 + p1)
    g1 = pg_top * p1 / (p0 + p1)

    @pl.when(t == 0)
    def _():
        carry_ref[...] = jnp.zeros_like(carry_ref)
    onehot = jnp.where((lane == i0) | (lane == i1), 1.0, 0.0)
    r = _iota((TM, TM), 0)
    c = _iota((TM, TM), 1)
    before = (c < r).astype(BF16)
    prefix = _dot(before, onehot.astype(BF16)) + carry_ref[0:1, :]
    rank0 = jnp.sum(jnp.where(lane == i0, prefix, 0.0), axis=1, keepdims=True)
    rank1 = jnp.sum(jnp.where(lane == i1, prefix, 0.0), axis=1, keepdims=True)
    total = carry_ref[0:1, :] + jnp.sum(onehot, axis=0, keepdims=True)
    carry_ref[...] = jnp.broadcast_to(total, carry_ref.shape)
    cnt_ref[...] = jnp.broadcast_to(total, cnt_ref.shape)

    vals = (i0.astype(F32) - E_LANE, i1.astype(F32) - E_LANE, g0, g1, rank0, rank1)
    route = jnp.zeros(lg.shape, F32)
    for n, val in enumerate(vals):
        route = jnp.where(lane == n, val, route)
    route_ref[...] = route


def _mix_out(alpha, o4, x, w_out, ln_g, ln_b, wr, rb):
    tp, d = x.shape
    grp_w = N_GROUP_HEADS * HEAD_DIM
    row = lambda t: (t, 0)
    fixed = lambda t: (0, 0)
    return pl.pallas_call(
        functools.partial(_mix_out_kernel, alpha),
        grid=(tp // TM,),
        in_specs=[pl.BlockSpec((TM, grp_w), row)] * 4 + [
            pl.BlockSpec((TM, d), row), pl.BlockSpec(w_out.shape, fixed),
            pl.BlockSpec((1, d), fixed), pl.BlockSpec((1, d), fixed),
            pl.BlockSpec(wr.shape, lambda t: (0, 0, 0)), pl.BlockSpec((1, LANES), fixed)],
        out_specs=(pl.BlockSpec((TM, d), row), pl.BlockSpec((TM, LANES), row), pl.BlockSpec((8, LANES), fixed)),
        out_shape=(jax.ShapeDtypeStruct((tp, d), F32), jax.ShapeDtypeStruct((tp, LANES), F32),
                   jax.ShapeDtypeStruct((8, LANES), F32)),
        scratch_shapes=[pltpu.VMEM((8, LANES), F32)],
        compiler_params=_cparams(("arbitrary",)),
        name="mix_out",
    )(*o4, x, w_out, ln_g, ln_b, wr, rb)


def _dispatch_kernel(dest_ref, h_ref, buf_in_ref, buf_ref, sem):
    del buf_in_ref
    t = pl.program_id(0)

    def copy(r, k):
        d = dest_ref[(t * TM + r) * 2 + k]
        return pltpu.make_async_copy(h_ref.at[pl.ds(r, 1), :], buf_ref.at[pl.ds(d, 1), :], sem)

    def start(r, _):
        copy(r, 0).start()
        copy(r, 1).start()
        return 0

    def wait(r, _):
        copy(r, 0).wait()
        copy(r, 1).wait()
        return 0

    lax.fori_loop(0, TM, start, 0)
    lax.fori_loop(0, TM, wait, 0)


def _dispatch(dest_flat, h, n_rows):
    tp, d = h.shape
    buf0 = jnp.zeros((n_rows, d), F32)
    return pl.pallas_call(
        _dispatch_kernel,
        grid_spec=pltpu.PrefetchScalarGridSpec(
            num_scalar_prefetch=1,
            grid=(tp // TM,),
            in_specs=[pl.BlockSpec((TM, d), lambda t, dest: (t, 0)), pl.BlockSpec(memory_space=pl.ANY)],
            out_specs=pl.BlockSpec(memory_space=pl.ANY),
            scratch_shapes=[pltpu.SemaphoreType.DMA(())]),
        out_shape=jax.ShapeDtypeStruct((n_rows, d), F32),
        input_output_aliases={2: 0},
        compiler_params=_cparams(("arbitrary",)),
        name="moe_dispatch",
    )(dest_flat, h, buf0)


def _expert_kernel(be_ref, nb_ref, x_ref, wgu_ref, wd_ref, o_ref):
    j = pl.program_id(0)

    @pl.when(j < nb_ref[0])
    def _():
        de = wd_ref.shape[1]
        xb = x_ref[...].astype(BF16)
        hid = _dot(xb, wgu_ref[0].astype(BF16))
        a, u = hid[:, :de], hid[:, de:]
        act = (a * (1.0 / (1.0 + jnp.exp(-a)))) * u
        o_ref[...] = _dot(act.astype(BF16), wd_ref[0].astype(BF16))

    @pl.when(j >= nb_ref[0])
    def _():
        o_ref[...] = jnp.zeros_like(o_ref)


def _experts(block_e, n_used, buf, w_gate_up, w_down):
    n_rows, d = buf.shape
    _, _, de2 = w_gate_up.shape
    return pl.pallas_call(
        _expert_kernel,
        grid_spec=pltpu.PrefetchScalarGridSpec(
            num_scalar_prefetch=2,
            grid=(n_rows // MOE_ROWS,),
            in_specs=[pl.BlockSpec((MOE_ROWS, d), lambda j, be, nb: (j, 0)),
                      pl.BlockSpec((1, d, de2), lambda j, be, nb: (be[j], 0, 0)),
                      pl.BlockSpec((1, de2 // 2, d), lambda j, be, nb: (be[j], 0, 0))],
            out_specs=pl.BlockSpec((MOE_ROWS, d), lambda j, be, nb: (j, 0))),
        out_shape=jax.ShapeDtypeStruct((n_rows, d), F32),
        compiler_params=_cparams(("arbitrary",)),
        name="moe_experts",
    )(block_e, n_used, buf, w_gate_up, w_down)


def _combine_kernel(alpha, dest_ref, h_ref, route_ref, g_ref, b_ref, ebuf_ref, y_ref, rows_ref, sem):
    t = pl.program_id(0)

    def copy(r, k):
        d = dest_ref[(t * TM + r) * 2 + k]
        return pltpu.make_async_copy(ebuf_ref.at[pl.ds(d, 1), :], rows_ref.at[k, pl.ds(r, 1), :], sem)

    def start(r, _):
        copy(r, 0).start()
        copy(r, 1).start()
        return 0

    def wait(r, _):
        copy(r, 0).wait()
        copy(r, 1).wait()
        return 0

    lax.fori_loop(0, TM, start, 0)
    lax.fori_loop(0, TM, wait, 0)
    route = route_ref[...]
    f = rows_ref[0] * route[:, 2:3] + rows_ref[1] * route[:, 3:4]
    y_ref[...] = _layer_norm(h_ref[...] * alpha + f, g_ref[...], b_ref[...])


def _combine(alpha, dest_flat, h, route, ln_g, ln_b, ebuf):
    tp, d = h.shape
    row = lambda t, dest: (t, 0)
    fixed = lambda t, dest: (0, 0)
    return pl.pallas_call(
        functools.partial(_combine_kernel, alpha),
        grid_spec=pltpu.PrefetchScalarGridSpec(
            num_scalar_prefetch=1,
            grid=(tp // TM,),
            in_specs=[pl.BlockSpec((TM, d), row), pl.BlockSpec((TM, LANES), row),
                      pl.BlockSpec((1, d), fixed), pl.BlockSpec((1, d), fixed),
                      pl.BlockSpec(memory_space=pl.ANY)],
            out_specs=pl.BlockSpec((TM, d), row),
            scratch_shapes=[pltpu.VMEM((2, TM, d), F32), pltpu.SemaphoreType.DMA(())]),
        out_shape=jax.ShapeDtypeStruct((tp, d), F32),
        compiler_params=_cparams(("arbitrary",)),
        name="moe_combine",
    )(dest_flat, h, route, ln_g, ln_b, ebuf)


def _moe(alpha, h, route, counts, w_gate_up, w_down, ln_g, ln_b):
    tp, _ = h.shape
    eid = route[:, 0:2].astype(I32)
    rank = route[:, 4:6].astype(I32)
    counts = counts.astype(I32)
    padded = (counts + MOE_ROWS - 1) // MOE_ROWS * MOE_ROWS
    pends = jnp.cumsum(padded)
    pstarts = pends - padded
    dest = (pstarts[eid] + rank).reshape(-1)
    n_blocks = -(-(tp * 2) // MOE_ROWS) + N_EXPERTS
    block_start = jnp.arange(n_blocks, dtype=I32) * MOE_ROWS
    block_e = jnp.minimum(jnp.sum(block_start[:, None] >= pends[None, :], axis=-1), N_EXPERTS - 1).astype(I32)
    n_used = (pends[-1:] // MOE_ROWS).astype(I32)
    buf = _dispatch(dest, h, n_blocks * MOE_ROWS)
    ebuf = _experts(block_e, n_used, buf, w_gate_up, w_down)
    return _combine(alpha, dest, h, route, ln_g, ln_b, ebuf)


def _round_up(n, m):
    return -(-n // m) * m


def _concat_rows(x_prompt, x_sample):
    d = x_prompt.shape[-1]
    xp = x_prompt.reshape(-1, d)
    xs = x_sample.reshape(-1, d)
    n = xp.shape[0] + xs.shape[0]
    pad = _round_up(n, TM) - n
    return jnp.concatenate([xp, xs, jnp.zeros((pad, d), xp.dtype)], axis=0)


def _tables(seq, dec_seq, past_len, dec_batch):
    n_s = dec_batch * dec_seq
    pos_s = past_len + (jnp.arange(_round_up(n_s, TM), dtype=I32) % dec_seq)
    return jnp.concatenate([_rope_tables(jnp.arange(seq, dtype=I32)), _rope_tables(pos_s)], axis=0)


def _prep_layer(p, l):
    w_in = p["w_in"][l]
    cols = w_in.shape[1]
    w_pad = jnp.pad(w_in, ((0, 0), (0, _round_up(cols, LANES) - cols))).astype(BF16)
    bf_row = jnp.zeros((1, LANES), F32).at[0, F_LANE:F_LANE + N_GROUP_HEADS].set(p["b_f"][l])
    lamv = jnp.zeros((8, LANES), F32)
    for n, name in enumerate(("lam_q1", "lam_k1", "lam_q2", "lam_k2")):
        lamv = lamv.at[n, :DIFF_HALF].set(p[name][l])
    subg2 = jnp.concatenate([p["sub_g"][l], p["sub_g"][l]])[None, :]
    wr = jnp.zeros((p["router_g"].shape[1], LANES), F32)
    wr = wr.at[:, :E_LANE].set(p["router_g"][l]).at[:, E_LANE:E_LANE + N_EXPERTS].set(p["router_e"][l])
    wr_hi = wr.astype(BF16)
    wr_lo = (wr - wr_hi.astype(F32)).astype(BF16)
    rb = jnp.zeros((1, LANES), F32)
    rb = rb.at[0, :E_LANE].set(p["router_g_b"][l]).at[0, E_LANE:E_LANE + N_EXPERTS].set(p["router_e_b"][l])
    return dict(w_in=w_pad, bf_row=bf_row, lamv=lamv, subg2=subg2, wr=jnp.stack([wr_hi, wr_lo]), rb=rb,
                w_out=p["w_out"][l].astype(BF16))


def kernel(x_prompt, x_sample, cache_k, cache_v, cache_idx, cache_logf, page_table, w_in, b_f, lam_q1, lam_k1,
           lam_q2, lam_k2, sub_g, w_out, ln1_g, ln1_b, router_g, router_g_b, router_e, router_e_b, w_gate_up,
           w_down, ln2_g, ln2_b):
    p = dict(cache_k=cache_k, cache_v=cache_v, cache_idx=cache_idx, cache_logf=cache_logf,
             page_table=page_table, w_in=w_in, b_f=b_f, lam_q1=lam_q1, lam_k1=lam_k1, lam_q2=lam_q2,
             lam_k2=lam_k2, sub_g=sub_g, w_out=w_out, router_g=router_g, router_g_b=router_g_b,
             router_e=router_e, router_e_b=router_e_b)
    depth = w_in.shape[0]
    n_seq, seq, d = x_prompt.shape
    db, dec_seq, _ = x_sample.shape
    t, n_s = n_seq * seq, db * dec_seq
    assert seq % TQ == 0 and t % TM == 0 and n_s <= TM and dec_seq <= T8
    assert page_table.shape[1] % CP == 0 and cache_k.shape[2] == PAGE
    alpha = (2 * depth) ** 0.25
    past_len = page_table.shape[1] * PAGE
    tabs = _tables(seq, dec_seq, past_len, db)
    mix = N_HEADS * HEAD_DIM
    p["cache_k"] = jnp.transpose(cache_k, (0, 1, 3, 4, 2)).reshape(cache_k.shape[0], cache_k.shape[1], mix, PAGE)
    p["cache_v"] = jnp.transpose(cache_v, (0, 1, 3, 4, 2)).reshape(cache_v.shape[0], cache_v.shape[1], mix, PAGE)
    p["cache_idx"] = jnp.swapaxes(cache_idx, 2, 3)
    p["cache_logf"] = jnp.pad(jnp.swapaxes(cache_logf, 2, 3), ((0, 0), (0, 0), (0, 8 - N_GROUP_HEADS), (0, 0)))

    y = _concat_rows(x_prompt, x_sample)
    tp = y.shape[0]
    rows_p, rows_s = [], []
    for l in range(depth):
        lam_init = 0.8 - 0.6 * math.exp(-0.3 * l)
        st = _prep_layer(p, l)
        qs, kb, vb, kf, vf, qib, misc, ki2b, vx = _project(y, st["w_in"], tabs, st["bf_row"], t // TM, seq // TM)
        fcol, ft = _fcum(misc, n_seq, seq)
        o_p = _prompt_attention(lam_init, qs, kb, vx, qib, ki2b, misc, fcol, ft, st["lamv"], st["subg2"],
                                n_seq, seq)
        o_s = _sample_attention(l, lam_init, p, tuple(a[t:] for a in (qs, kb, vb, qib, misc)),
                                st["lamv"], st["subg2"], db, dec_seq)
        grp_w = N_GROUP_HEADS * HEAD_DIM
        o4 = [jnp.concatenate([o_p[g], o_s[:, g * grp_w:(g + 1) * grp_w],
                               jnp.zeros((tp - t - n_s, grp_w), BF16)], axis=0) for g in range(4)]
        h, route, cnt = _mix_out(alpha, o4, y, st["w_out"], ln1_g[l][None, :], ln1_b[l][None, :],
                                 st["wr"], st["rb"])
        y = _moe(alpha, h, route, cnt[0, E_LANE:E_LANE + N_EXPERTS], w_gate_up[l], w_down[l],
                 ln2_g[l][None, :], ln2_b[l][None, :])
        rows_p.append((kf[:t], vf[:t], misc[:t, :IDX_DIM], misc[:t, F_LANE:F_LANE + N_GROUP_HEADS]))
        rows_s.append((kf[t:t + n_s], vf[t:t + n_s], misc[t:t + n_s, :IDX_DIM],
                       misc[t:t + n_s, F_LANE:F_LANE + N_GROUP_HEADS]))

    def stack(rows, n, lead):
        tail = {0: (N_HEADS, HEAD_DIM), 1: (N_HEADS, HEAD_DIM), 2: (IDX_DIM,), 3: (N_GROUP_HEADS,)}[n]
        return jnp.stack([r[n].reshape(*lead, *tail) for r in rows])

    outs = [y[:t].reshape(n_seq, seq, d), y[t:t + n_s].reshape(db, dec_seq, d)]
    outs += [stack(rows_p, n, (n_seq, seq)) for n in range(4)]
    outs += [stack(rows_s, n, (db, dec_seq)) for n in range(4)]
    return tuple(outs)
```

```python
import functools
import math

import jax
import jax.numpy as jnp
import numpy as np
from jax import lax
from jax.experimental import pallas as pl
from jax.experimental.pallas import tpu as pltpu

F32 = jnp.float32
BF16 = jnp.bfloat16
I32 = jnp.int32

HEAD_DIM = 64
N_HEADS = 16
N_GROUP_HEADS = 4
DIFF_HALF = 32
IDX_HEADS = 8
IDX_DIM = 64
DSA_TOPK = 256
ROPE_THETA = 10000.0
N_EXPERT_GROUPS = 4
EXPERTS_PER_GROUP = 8
N_EXPERTS = 32
LN_EPS = 1e-5

LANES = 128
VMEM_LIMIT = 56 * 1024 * 1024

TM = 256
TQ = 256
MOE_ROWS = 256
NEG = -1e30
INT_MIN = -2 ** 31

WI_LANE = IDX_DIM
F_LANE = IDX_DIM + IDX_HEADS


def _cparams(sem):
    return pltpu.CompilerParams(dimension_semantics=sem, vmem_limit_bytes=VMEM_LIMIT)


def _dot(a, b):
    return jnp.dot(a, b, preferred_element_type=F32)


def _dot_nt(a, b):
    return lax.dot_general(a, b, (((1,), (1,)), ((), ())), preferred_element_type=F32)


def _split2(x):
    hi = x.astype(BF16)
    lo = (x - hi.astype(F32)).astype(BF16)
    return hi, lo


def _split3(x):
    hi = x.astype(BF16)
    r = x - hi.astype(F32)
    mid = r.astype(BF16)
    lo = (r - mid.astype(F32)).astype(BF16)
    return hi, mid, lo


def _log_sigmoid(z):
    return jnp.minimum(z, 0.0) - jnp.log1p(jnp.exp(-jnp.abs(z)))


def _iota(shape, dim):
    return lax.broadcasted_iota(I32, shape, dim)


def _rope_chunk(y, cos, sin_signed, half):
    lane = _iota(y.shape, 1)
    first = (lane % (2 * half)) < half
    fwd = pltpu.roll(y, LANES - half, 1)
    bwd = pltpu.roll(y, half, 1)
    return y * cos + jnp.where(first, fwd, bwd) * sin_signed


def _proj_kernel(x_ref, w_ref, tab_ref, bf_ref, *refs):
    qs_ref, kb_ref, vb_ref, kf_ref, vf_ref, qib_ref, misc_ref, ki2_ref, vx_ref = refs[-9:]
    xb = x_ref[...].astype(BF16)
    cos_d, sin_d = tab_ref[:, 0:128], tab_ref[:, 128:256]
    cos_f, sin_f = tab_ref[:, 256:384], tab_ref[:, 384:512]
    mix = N_HEADS * HEAD_DIM
    wide = 2 * LANES

    def chunk2(c2):
        y = _dot(xb, w_ref[:, c2 * wide:(c2 + 1) * wide])
        return y[:, :LANES], y[:, LANES:]

    def roped(c, y):
        if c < 2:
            return _rope_chunk(y, cos_d, sin_d, DIFF_HALF // 2)
        if c < 4:
            return _rope_chunk(y, cos_f, sin_f, HEAD_DIM // 2)
        return y

    nch = mix // LANES
    n2 = nch // 2
    for c2 in range(n2):
        q2, k2, v2 = chunk2(c2), chunk2(n2 + c2), chunk2(2 * n2 + c2)
        for half in range(2):
            c = 2 * c2 + half
            q = roped(c, q2[half])
            scale = DIFF_HALF ** -0.5 if c < 2 else HEAD_DIM ** -0.5
            qs_ref[:, c * LANES:(c + 1) * LANES] = (q * scale).astype(BF16)
            k = roped(c, k2[half])
            kf_ref[0, :, c * LANES:(c + 1) * LANES] = k
            kb_ref[:, c * LANES:(c + 1) * LANES] = k.astype(BF16)
            v = v2[half]
            vf_ref[0, :, c * LANES:(c + 1) * LANES] = v
            vb_ref[:, c * LANES:(c + 1) * LANES] = v.astype(BF16)
            low = _iota(v.shape, 1) < HEAD_DIM
            vx_ref[:, 2 * c * LANES:(2 * c + 1) * LANES] = jnp.where(low, v, 1.0).astype(BF16)
            vx_ref[:, (2 * c + 1) * LANES:(2 * c + 2) * LANES] = jnp.where(
                low, pltpu.roll(v, HEAD_DIM, 1), 1.0).astype(BF16)
    for c2 in range(IDX_HEADS * IDX_DIM // wide):
        for half, y in enumerate(chunk2(3 * n2 + c2)):
            c = 2 * c2 + half
            qi = _rope_chunk(y, cos_f, sin_f, IDX_DIM // 2)
            qib_ref[:, c * LANES:(c + 1) * LANES] = qi.astype(BF16)
    misc_col = 3 * mix + IDX_HEADS * IDX_DIM
    raw = _dot(xb, w_ref[:, misc_col:misc_col + LANES])
    ki = _rope_chunk(raw, cos_f, sin_f, IDX_DIM // 2)
    lane = _iota(raw.shape, 1)
    wi_scale = IDX_DIM ** -0.5 * IDX_HEADS ** -0.5
    logf = _log_sigmoid(raw + bf_ref[...])
    misc = jnp.where(lane < IDX_DIM, ki,
                     jnp.where(lane < F_LANE, raw * wi_scale,
                               jnp.where(lane < F_LANE + N_GROUP_HEADS, logf, 0.0)))
    misc_ref[...] = misc
    ki2_ref[...] = jnp.where(lane < IDX_DIM, ki, pltpu.roll(ki, IDX_DIM, 1)).astype(BF16)


def _project(x, w_pad, tab, bf_row, tile0, n_tiles, tab_map, layer, depth, kv_prev=None):
    _, d = x.shape
    mix = N_HEADS * HEAD_DIM
    n = n_tiles * TM
    row = lambda t: (t, 0)
    fixed = lambda t: (0, 0)
    slab = pl.BlockSpec((1, TM, mix), lambda t: (layer, t, 0))
    out_shapes = (
        jax.ShapeDtypeStruct((n, mix), BF16), jax.ShapeDtypeStruct((n, mix), BF16),
        jax.ShapeDtypeStruct((n, mix), BF16), jax.ShapeDtypeStruct((depth, n, mix), F32),
        jax.ShapeDtypeStruct((depth, n, mix), F32), jax.ShapeDtypeStruct((n, IDX_HEADS * IDX_DIM), BF16),
        jax.ShapeDtypeStruct((n, LANES), F32), jax.ShapeDtypeStruct((n, LANES), BF16),
        jax.ShapeDtypeStruct((n, 2 * mix), BF16))
    in_specs = [pl.BlockSpec((TM, d), lambda t: (tile0 + t, 0)), pl.BlockSpec(w_pad.shape, fixed),
                pl.BlockSpec((TM, 512), tab_map), pl.BlockSpec((1, LANES), fixed)]
    args = [x, w_pad, tab, bf_row]
    aliases = {}
    if kv_prev is not None:
        in_specs += [pl.BlockSpec(memory_space=pl.ANY)] * 2
        args += list(kv_prev)
        aliases = {4: 3, 5: 4}
    return pl.pallas_call(
        _proj_kernel,
        grid=(n_tiles,),
        in_specs=in_specs,
        out_specs=(pl.BlockSpec((TM, mix), row), pl.BlockSpec((TM, mix), row), pl.BlockSpec((TM, mix), row),
                   slab, slab,
                   pl.BlockSpec((TM, IDX_HEADS * IDX_DIM), row), pl.BlockSpec((TM, LANES), row),
                   pl.BlockSpec((TM, LANES), row), pl.BlockSpec((TM, 2 * mix), row)),
        out_shape=out_shapes,
        input_output_aliases=aliases,
        compiler_params=_cparams(("parallel",)),
        name="proj",
    )(*args)


def _rope_tables(pos):
    def pattern(half):
        lane = jnp.arange(LANES)
        inv = ROPE_THETA ** (-(lane % half).astype(F32) / half)
        ang = pos.astype(F32)[:, None] * inv[None, :]
        sign = jnp.where((lane % (2 * half)) < half, -1.0, 1.0)
        return jnp.cos(ang), jnp.sin(ang) * sign[None, :]
    cd, sd = pattern(DIFF_HALF // 2)
    cf, sf = pattern(HEAD_DIM // 2)
    return jnp.concatenate([cd, sd, cf, sf], axis=1)


def _fcum_kernel(misc_ref, fcol_ref, ft_ref):
    s = misc_ref.shape[0]
    r = _iota((TQ, TQ), 0)
    c = _iota((TQ, TQ), 1)
    tri = (r >= c).astype(BF16)
    sel = (_iota((8, LANES), 1) == _iota((8, LANES), 0) + F_LANE).astype(BF16)
    carry = jnp.zeros((1, LANES), F32)
    for b in range(s // TQ):
        seg = misc_ref[b * TQ:(b + 1) * TQ, :]
        cs = carry
        for piece in _split3(seg):
            cs = cs + _dot(tri, piece)
        fcol_ref[b * TQ:(b + 1) * TQ, :] = cs
        tr = jnp.zeros((8, TQ), F32)
        for piece in _split3(cs):
            tr = tr + _dot_nt(sel, piece)
        ft_ref[b] = tr
        carry = cs[TQ - 1:TQ, :]


def _fcum(misc, n_seq, seq):
    nkb = seq // TQ
    return pl.pallas_call(
        _fcum_kernel,
        grid=(n_seq,),
        in_specs=[pl.BlockSpec((seq, LANES), lambda b: (b, 0))],
        out_specs=(pl.BlockSpec((seq, LANES), lambda b: (b, 0)),
                   pl.BlockSpec((nkb, 8, TQ), lambda b: (b, 0, 0))),
        out_shape=(jax.ShapeDtypeStruct((n_seq * seq, LANES), F32),
                   jax.ShapeDtypeStruct((n_seq * nkb, 8, TQ), F32)),
        compiler_params=_cparams(("parallel",)),
        name="fcum",
    )(misc)


def _half_mask(width, offset, size):
    lane = _iota((1, LANES), 1)
    return (lane >= offset) & (lane < offset + size)


def _masked(q, offset, size):
    return jnp.where(_half_mask(LANES, offset, size), q, jnp.zeros_like(q))


def _kblock(ref, j, width, col):
    return ref[pl.ds(pl.multiple_of(j * TQ, TQ), TQ), col * width:(col + 1) * width]


def _head_q(q_ref, hh, offset=0, size=HEAD_DIM):
    qp = q_ref[:, (hh // 2) * LANES:(hh // 2 + 1) * LANES]
    return _masked(qp, (hh % 2) * HEAD_DIM + offset, size)


def _diag_mask(strict=False):
    row = _iota((TQ, TQ), 0)
    col = _iota((TQ, TQ), 1)
    return (col < row) if strict else (col <= row)


def _softmax_reset(m_ref, acc_ref):
    m_ref[...] = jnp.full(m_ref.shape, NEG, F32)
    acc_ref[...] = jnp.zeros_like(acc_ref)


def _wide(x):
    return jnp.concatenate([x] * (TQ // LANES), axis=1)


def _row_bcast(col):
    return jnp.broadcast_to(col, (TQ, LANES))


def _softmax_block(n, s, vx, m_ref, acc_ref):
    m_old = m_ref[n]
    m_new = jnp.maximum(m_old, _row_bcast(jnp.max(s, axis=1, keepdims=True)))
    alpha = jnp.exp(m_old - m_new)
    p = jnp.exp(s - _wide(m_new))
    acc_ref[n] = alpha * acc_ref[n] + _dot(p.astype(BF16), vx)
    m_ref[n] = m_new


def _softmax_result(n, acc_ref):
    acc = acc_ref[n]
    return acc / pltpu.roll(acc, HEAD_DIM, 1)


def _store_heads(o_ref, outs):
    lane = _iota((TQ, LANES), 1)
    for pair in range(2):
        o = jnp.where(lane < HEAD_DIM, outs[2 * pair], pltpu.roll(outs[2 * pair + 1], HEAD_DIM, 1))
        o_ref[:, pair * LANES:(pair + 1) * LANES] = o.astype(o_ref.dtype)


def _attn_diff_kernel(lam_init, q_ref, k_ref, vx_ref, lamv_ref, subg_ref, o_ref, m_ref, acc_ref):
    i = pl.program_id(1)
    lv = lamv_ref[...]
    lam = (jnp.exp(jnp.sum(lv[0:1] * lv[1:2], axis=1, keepdims=True))
           - jnp.exp(jnp.sum(lv[2:3] * lv[3:4], axis=1, keepdims=True))) + lam_init
    _softmax_reset(m_ref, acc_ref)
    qs = [_head_q(q_ref, hh, part * DIFF_HALF, DIFF_HALF) for hh in range(N_GROUP_HEADS) for part in range(2)]

    def block(j, mask):
        for hh in range(N_GROUP_HEADS):
            kp = _kblock(k_ref, j, LANES, hh // 2)
            vx = _kblock(vx_ref, j, LANES, hh)
            for part in range(2):
                s = _dot_nt(qs[2 * hh + part], kp)
                if mask is not None:
                    s = jnp.where(mask, s, NEG)
                _softmax_block(2 * hh + part, s, vx, m_ref, acc_ref)

    def body(j, carry):
        block(j, None)
        return carry
    lax.fori_loop(0, i, body, 0)
    block(i, _diag_mask())

    outs = []
    lane = _iota((TQ, LANES), 1)
    for hh in range(N_GROUP_HEADS):
        oa = _softmax_result(2 * hh, acc_ref) - lam * _softmax_result(2 * hh + 1, acc_ref)
        ms = jnp.sum(jnp.where(lane < HEAD_DIM, oa * oa, 0.0), axis=1, keepdims=True) / HEAD_DIM
        outs.append(oa * lax.rsqrt(ms + LN_EPS) * subg_ref[...] * (1.0 - lam_init))
    _store_heads(o_ref, outs)


def _attn_fox_kernel(q_ref, k_ref, vx_ref, fcol_ref, ft_ref, o_ref, m_ref, acc_ref):
    i = pl.program_id(1)
    _softmax_reset(m_ref, acc_ref)
    qs = [_head_q(q_ref, hh) for hh in range(N_GROUP_HEADS)]
    fqs = [_row_bcast(fcol_ref[:, F_LANE + g:F_LANE + g + 1]) for g in range(N_GROUP_HEADS)]

    def block(j, mask):
        ft = ft_ref[j]
        for hh in range(N_GROUP_HEADS):
            s = _dot_nt(qs[hh], _kblock(k_ref, j, LANES, hh // 2)) + (_wide(fqs[hh]) - ft[hh:hh + 1, :])
            if mask is not None:
                s = jnp.where(mask, s, NEG)
            _softmax_block(hh, s, _kblock(vx_ref, j, LANES, hh), m_ref, acc_ref)

    def body(j, carry):
        block(j, None)
        return carry
    lax.fori_loop(0, i, body, 0)
    block(i, _diag_mask())
    _store_heads(o_ref, [_softmax_result(hh, acc_ref) for hh in range(N_GROUP_HEADS)])


def _attn_sb_kernel(q_ref, k_ref, vx_ref, o_ref, c_ref, acc_ref):
    i = pl.program_id(1)
    suffix = (_iota((TQ, TQ), 0) >= _iota((TQ, TQ), 1)).astype(BF16)
    c_ref[...] = jnp.zeros_like(c_ref)
    acc_ref[...] = jnp.zeros_like(acc_ref)
    qs = [_head_q(q_ref, hh) for hh in range(N_GROUP_HEADS)]

    def block(j, mask):
        for hh in range(N_GROUP_HEADS):
            z = _dot_nt(qs[hh], _kblock(k_ref, j, LANES, hh // 2))
            ls = jnp.minimum(z, 0.0) - jnp.log(1.0 + jnp.exp(-jnp.abs(z)))
            l1m = ls - z
            if mask is not None:
                l1m = jnp.where(mask, l1m, 0.0)
            hi, lo = _split2(l1m)
            incl = _dot(hi, suffix) + _dot(lo, suffix)
            carry = c_ref[hh]
            w = jnp.exp(ls + (incl - l1m + _wide(carry)))
            if mask is not None:
                w = jnp.where(mask, w, 0.0)
            acc_ref[hh] = acc_ref[hh] + _dot(w.astype(BF16), _kblock(vx_ref, j, LANES, hh))
            c_ref[hh] = carry + _row_bcast(incl[:, 0:1])

    block(i, _diag_mask(strict=True))

    def body(jj, carry):
        block(i - 1 - jj, None)
        return carry
    lax.fori_loop(0, i, body, 0)
    _store_heads(o_ref, [acc_ref[hh] for hh in range(N_GROUP_HEADS)])


def _sort_key(x):
    bits = pltpu.bitcast(x, I32)
    return bits ^ ((bits >> 31) & jnp.int32(0x7FFFFFFF))


def _topk_threshold(count_ge, n_sel, shape, steps=32):
    def body(b, tau):
        inc = lax.shift_left(jnp.int32(1), jnp.int32(31) - b)
        cand = tau + inc
        return jnp.where(count_ge(cand) >= n_sel, cand, tau)
    return lax.fori_loop(0, steps, body, jnp.full(shape, INT_MIN, I32))


def _attn_dsa_kernel(n_sel, q_ref, k_ref, vx_ref, qi_ref, ki2_ref, misc_ref, o_ref, key_ref, sel_ref,
                     m_ref, acc_ref):
    i = pl.program_id(1)
    nb = i + 1

    qis = [_masked(qi_ref[:, (ih // 2) * LANES:(ih // 2 + 1) * LANES], (ih % 2) * IDX_DIM, IDX_DIM)
           for ih in range(IDX_HEADS)]
    ws = [misc_ref[:, WI_LANE + ih:WI_LANE + ih + 1] for ih in range(IDX_HEADS)]

    def scores(j):
        kip = _kblock(ki2_ref, j, LANES, 0)
        isc = jnp.zeros((TQ, TQ), F32)
        for ih in range(IDX_HEADS):
            isc = isc + jnp.maximum(_dot_nt(qis[ih], kip), 0.0) * ws[ih]
        return _sort_key(isc)

    def score_block(j, carry):
        key_ref[j] = scores(j)
        return carry
    lax.fori_loop(0, i, score_block, 0)
    key_ref[i] = jnp.where(_diag_mask(), scores(i), INT_MIN)

    def count(pred):
        def body(j, acc):
            key = key_ref[j]
            for c in range(TQ // LANES):
                acc = acc + jnp.where(pred(key[:, c * LANES:(c + 1) * LANES]), 1.0, 0.0)
            return acc
        acc = lax.fori_loop(0, nb, body, jnp.zeros((TQ, LANES), F32))
        return _row_bcast(jnp.sum(acc, axis=1, keepdims=True))

    steps = jnp.where((i == 0) & (n_sel >= TQ), 0, 32)
    tau = _topk_threshold(lambda cand: count(lambda k: k >= cand), float(n_sel), (TQ, LANES), steps)
    need = _wide(float(n_sel) - count(lambda k: k > tau))
    tau = _wide(tau)

    before = (_iota((TQ, TQ), 0) < _iota((TQ, TQ), 1)).astype(BF16)
    ones = jnp.ones((TQ, LANES), BF16)

    def select_block(j, seen):
        key = key_ref[j]
        eq = (key == tau) & (key != INT_MIN)
        eqb = jnp.where(eq, 1.0, 0.0).astype(BF16)
        prefix = _dot(eqb, before) + _wide(seen)
        sel = (key > tau) | (eq & (prefix < need))
        sel_ref[j] = jnp.where(sel, 0.0, NEG)
        return seen + _dot(eqb, ones)
    lax.fori_loop(0, nb, select_block, jnp.zeros((TQ, LANES), F32))

    _softmax_reset(m_ref, acc_ref)
    qs = [_head_q(q_ref, hh) for hh in range(N_GROUP_HEADS)]

    def body(j, carry):
        bias = sel_ref[j]
        for hh in range(N_GROUP_HEADS):
            s = _dot_nt(qs[hh], _kblock(k_ref, j, LANES, hh // 2)) + bias
            _softmax_block(hh, s, _kblock(vx_ref, j, LANES, hh), m_ref, acc_ref)
        return carry
    lax.fori_loop(0, nb, body, 0)
    _store_heads(o_ref, [_softmax_result(hh, acc_ref) for hh in range(N_GROUP_HEADS)])


def _prompt_attention(lam_init, qs, kb, vx, qib, ki2b, misc, fcol, ft, lamv, subg2, n_seq, seq):
    nq = seq // TQ
    nkb = seq // TQ
    t = n_seq * seq
    grp = N_GROUP_HEADS * HEAD_DIM

    def qspec(g):
        return pl.BlockSpec((TQ, grp), lambda b, i: (b * nq + i, g))

    def kspec(g):
        return pl.BlockSpec((seq, grp), lambda b, i: (b, g))

    def vspec(g):
        return pl.BlockSpec((seq, 2 * grp), lambda b, i: (b, g))

    ospec = pl.BlockSpec((TQ, grp), lambda b, i: (b * nq + i, 0))
    oshape = jax.ShapeDtypeStruct((t, grp), BF16)
    fixed = lambda b, i: (0, 0)
    rowblk = lambda b, i: (b * nq + i, 0)
    params = _cparams(("parallel", "arbitrary"))

    def state(n):
        return [pltpu.VMEM((n, TQ, LANES), F32), pltpu.VMEM((n, TQ, LANES), F32)]

    o_a = pl.pallas_call(
        functools.partial(_attn_diff_kernel, lam_init), grid=(n_seq, nq),
        in_specs=[qspec(0), kspec(0), vspec(0), pl.BlockSpec((8, LANES), fixed),
                  pl.BlockSpec((1, LANES), fixed)],
        out_specs=ospec, out_shape=oshape, scratch_shapes=state(2 * N_GROUP_HEADS),
        compiler_params=params, name="attn_diff",
    )(qs, kb, vx, lamv, subg2)
    n_sel = min(DSA_TOPK, seq // 4)
    o_b = pl.pallas_call(
        functools.partial(_attn_dsa_kernel, n_sel), grid=(n_seq, nq),
        in_specs=[qspec(1), kspec(1), vspec(1),
                  pl.BlockSpec((TQ, IDX_HEADS * IDX_DIM), rowblk),
                  pl.BlockSpec((seq, LANES), lambda b, i: (b, 0)),
                  pl.BlockSpec((TQ, LANES), rowblk)],
        out_specs=ospec, out_shape=oshape,
        scratch_shapes=[pltpu.VMEM((nkb, TQ, TQ), I32), pltpu.VMEM((nkb, TQ, TQ), F32)] + state(N_GROUP_HEADS),
        compiler_params=params, name="attn_dsa",
    )(qs, kb, vx, qib, ki2b, misc)
    o_c = pl.pallas_call(
        _attn_sb_kernel, grid=(n_seq, nq),
        in_specs=[qspec(2), kspec(2), vspec(2)],
        out_specs=ospec, out_shape=oshape, scratch_shapes=state(N_GROUP_HEADS),
        compiler_params=params, name="attn_sb",
    )(qs, kb, vx)
    o_d = pl.pallas_call(
        _attn_fox_kernel, grid=(n_seq, nq),
        in_specs=[qspec(3), kspec(3), vspec(3), pl.BlockSpec((TQ, LANES), rowblk),
                  pl.BlockSpec((nkb, 8, TQ), lambda b, i: (b, 0, 0))],
        out_specs=ospec, out_shape=oshape, scratch_shapes=state(N_GROUP_HEADS),
        compiler_params=params, name="attn_fox",
    )(qs, kb, vx, fcol, ft)
    return o_a, o_b, o_c, o_d


PAGE = 128
CP = 4
CK = CP * PAGE
DEC_HALVES = 2
T8 = 8
SLOT_A1, SLOT_A2, SLOT_B, SLOT_C, SLOT_D = range(5)
N_QROWS = 5 * N_GROUP_HEADS * T8


def _dec_index_kernel(n_sel, dec_seq, pt_ref, *refs):
    ki_refs = refs[:CP]
    lf_refs = refs[CP:2 * CP]
    qi_ref, w_ref, kin_ref, selp_ref, seln_ref, dsuf_ref, key_ref, carry_ref = refs[2 * CP:]
    c = pl.program_id(1)
    nch = pl.num_programs(1)

    def scores(s):
        s = jnp.maximum(s, 0.0) * w_ref[0][:, 0:1]
        isc = jnp.zeros((T8, s.shape[1]), F32)
        for ih in range(IDX_HEADS):
            isc = isc + s[ih * T8:(ih + 1) * T8, :]
        return isc

    kit = jnp.concatenate([r[0, 0] for r in ki_refs], axis=1).astype(BF16)
    key_ref[nch - 1 - c] = _sort_key(scores(_dot(qi_ref[0], kit)))

    @pl.when(c == 0)
    def _():
        carry_ref[...] = jnp.zeros_like(carry_ref)
    lf = jnp.concatenate([r[0, 0] for r in lf_refs], axis=1)
    after = (_iota((CK, CK), 0) > _iota((CK, CK), 1)).astype(BF16)
    carry = carry_ref[:, 0:1]
    suf = carry
    for piece in _split3(lf):
        suf = suf + _dot(piece, after)
    dsuf_ref[0, 0] = suf
    carry_ref[...] = jnp.broadcast_to(carry + jnp.sum(lf, axis=1, keepdims=True), carry_ref.shape)

    @pl.when(c == nch - 1)
    def _():
        t8 = _iota((T8, PAGE), 0)
        j = _iota((T8, PAGE), 1)
        keyn = jnp.where((j <= t8) & (j < dec_seq),
                         _sort_key(scores(_dot_nt(qi_ref[0], kin_ref[0].astype(BF16)))), INT_MIN)
        n_blk = key_ref.shape[0]

        def count(pred):
            def body(b, acc):
                return acc + jnp.where(pred(key_ref[b]), 1.0, 0.0)
            acc = lax.fori_loop(0, n_blk, body, jnp.zeros((T8, CK), F32))
            return (jnp.sum(acc, axis=1, keepdims=True)
                    + jnp.sum(jnp.where(pred(keyn), 1.0, 0.0), axis=1, keepdims=True))

        tau = _topk_threshold(lambda cand: count(lambda k: k >= cand), float(n_sel), (T8, 1))
        need = float(n_sel) - count(lambda k: k > tau)
        before = (_iota((CK, CK), 0) < _iota((CK, CK), 1)).astype(BF16)

        def select(key, seen, tri):
            eq = (key == tau) & (key != INT_MIN)
            eqf = jnp.where(eq, 1.0, 0.0)
            prefix = _dot(eqf.astype(BF16), tri) + seen
            sel = (key > tau) | (eq & (prefix < need))
            return jnp.where(sel, 0.0, NEG), seen + jnp.sum(eqf, axis=1, keepdims=True)

        def body(b, seen):
            bias, seen = select(key_ref[b], seen, before)
            selp_ref[0, b] = bias
            return seen
        seen = lax.fori_loop(0, n_blk, body, jnp.zeros((T8, 1), F32))
        bias, _ = select(keyn, seen, before[:PAGE, :PAGE])
        seln_ref[0] = bias


def _dec_index(l, n_sel, dec_seq, pt_flat, idx_t, logf_t, qim, wv, ki_new, n_pages):
    db = qim.shape[0]
    nch = n_pages // CP

    def page_spec(u, rows):
        return pl.BlockSpec((1, 1, rows, PAGE),
                            lambda b, c, pt: (l, pt[b * n_pages + (nch - 1 - c) * CP + u], 0, 0))

    per_b = lambda b, c, pt: (b, 0, 0)
    chunk = lambda b, c, pt: (b, nch - 1 - c, 0, 0)
    return pl.pallas_call(
        functools.partial(_dec_index_kernel, n_sel, dec_seq),
        grid_spec=pltpu.PrefetchScalarGridSpec(
            num_scalar_prefetch=1,
            grid=(db, nch),
            in_specs=[page_spec(u, IDX_DIM) for u in range(CP)] + [page_spec(u, 8) for u in range(CP)] + [
                pl.BlockSpec((1, IDX_HEADS * T8, IDX_DIM), per_b),
                pl.BlockSpec((1, IDX_HEADS * T8, LANES), per_b),
                pl.BlockSpec((1, PAGE, IDX_DIM), per_b)],
            out_specs=(pl.BlockSpec((1, nch, T8, CK), lambda b, c, pt: (b, 0, 0, 0)),
                       pl.BlockSpec((1, T8, PAGE), per_b),
                       pl.BlockSpec((1, 1, 8, CK), chunk)),
            scratch_shapes=[pltpu.VMEM((nch, T8, CK), I32), pltpu.VMEM((8, LANES), F32)]),
        out_shape=(jax.ShapeDtypeStruct((db, nch, T8, CK), F32), jax.ShapeDtypeStruct((db, T8, PAGE), F32),
                   jax.ShapeDtypeStruct((db, nch, 8, CK), F32)),
        compiler_params=_cparams(("parallel", "arbitrary")),
        name="dec_index",
    )(pt_flat, *([idx_t] * CP), *([logf_t] * CP), qim, wv, ki_new)


def _slot_rows(slot, hh):
    return (slot * N_GROUP_HEADS + hh) * T8


def _dec_attn_kernel(lam_init, dec_seq, pt_ref, *refs):
    n_pg = DEC_HALVES * CP
    k_refs = refs[:n_pg]
    v_refs = refs[n_pg:2 * n_pg]
    (q_ref, kn_ref, vn_ref, selp_ref, seln_ref, dsuf_ref, miscn_ref, lamv_ref, subg_ref,
     o_ref, m_ref, l_ref, cc_ref, acc_ref) = refs[2 * n_pg:]
    c = pl.program_id(1)
    nch = pl.num_programs(1)
    rows = _iota((N_QROWS, 1), 0)
    is_c = (rows >= _slot_rows(SLOT_C, 0)) & (rows < _slot_rows(SLOT_D, 0))
    c0, c1 = _slot_rows(SLOT_C, 0), _slot_rows(SLOT_D, 0)
    b0 = _slot_rows(SLOT_B, 0)
    q = q_ref[0]

    lfn = miscn_ref[0]
    t8 = _iota((T8, LANES), 0)
    cn = jnp.zeros((T8, LANES), F32)
    for j in range(dec_seq):
        cn = cn + jnp.where(t8 >= j, lfn[j:j + 1, :], 0.0)

    def update(s, bias, ok_c, pv, suffix):
        m_old, l_old, carry = m_ref[:, 0:1], l_ref[:, 0:1], cc_ref[:, 0:1]
        sm = s + bias
        m_new = jnp.maximum(m_old, jnp.max(sm, axis=1, keepdims=True))
        alpha = jnp.exp(m_old - m_new)
        p = jnp.exp(sm - m_new)
        l_new = alpha * l_old + jnp.sum(p, axis=1, keepdims=True)
        z = s[c0:c1]
        ls = _log_sigmoid(z)
        l1m = jnp.where(ok_c, ls - z, 0.0)
        hi, lo = _split2(l1m)
        incl = _dot(hi, suffix) + _dot(lo, suffix)
        tail = incl - l1m + carry
        w = jnp.where(ok_c, jnp.exp(ls + tail), 0.0)
        pw = jnp.concatenate([p[:c0], w, p[c1:]], axis=0).astype(BF16)
        acc_ref[...] = jnp.where(is_c, 1.0, alpha) * acc_ref[...] + pv(pw)
        m_ref[...] = jnp.broadcast_to(m_new, m_ref.shape)
        l_ref[...] = jnp.broadcast_to(l_new, l_ref.shape)
        cc_ref[...] = jnp.broadcast_to(carry + incl[:, 0:1], cc_ref.shape)

    @pl.when(c == 0)
    def _():
        m_ref[...] = jnp.full(m_ref.shape, NEG, F32)
        l_ref[...] = jnp.zeros_like(l_ref)
        cc_ref[...] = jnp.zeros_like(cc_ref)
        acc_ref[...] = jnp.zeros_like(acc_ref)
        kn, vn = kn_ref[0], vn_ref[0]
        s = _dot_nt(q, kn)
        jn = _iota((N_QROWS, PAGE), 1)
        tn = _iota((N_QROWS, PAGE), 0) % T8
        live = jn < dec_seq
        sel_g = (_iota((8, LANES), 1) == _iota((8, LANES), 0) + F_LANE).astype(BF16)
        cn_pad = jnp.concatenate([cn, jnp.zeros((PAGE - T8, LANES), F32)], axis=0)
        cnt = jnp.zeros((8, PAGE), F32)
        for piece in _split3(cn_pad):
            cnt = cnt + _dot_nt(sel_g, piece)
        blocks = [jnp.zeros((b0, PAGE), F32)]
        blocks += [seln_ref[0]] * N_GROUP_HEADS
        blocks += [jnp.zeros((c1 - c0, PAGE), F32)]
        for g in range(N_GROUP_HEADS):
            blocks.append(cn[:, F_LANE + g:F_LANE + g + 1] - cnt[g:g + 1, :])
        bias = jnp.where(live & (jn <= tn), jnp.concatenate(blocks, axis=0), NEG)
        ok_c = (live & (jn < tn))[c0:c1]
        suffix = (_iota((PAGE, PAGE), 0) >= _iota((PAGE, PAGE), 1)).astype(BF16)
        update(s, bias, ok_c, lambda pw: _dot(pw, vn), suffix)

    suffix = (_iota((CK, CK), 0) >= _iota((CK, CK), 1)).astype(BF16)
    for half in reversed(range(DEC_HALVES)):
        kt = jnp.concatenate([r[0, 0].astype(BF16) for r in k_refs[half * CP:(half + 1) * CP]], axis=1)
        vt = jnp.concatenate([r[0, 0].astype(BF16) for r in v_refs[half * CP:(half + 1) * CP]], axis=1)
        s = _dot(q, kt)
        blocks = [jnp.zeros((b0, CK), F32)]
        blocks += [selp_ref[0, half]] * N_GROUP_HEADS
        blocks += [jnp.zeros((c1 - c0, CK), F32)]
        dsuf = dsuf_ref[0, half]
        for g in range(N_GROUP_HEADS):
            blocks.append(cn[:, F_LANE + g:F_LANE + g + 1] + dsuf[g:g + 1, :])
        bias = jnp.concatenate(blocks, axis=0)
        update(s, bias, jnp.full((c1 - c0, CK), True), lambda pw, vt=vt: _dot_nt(pw, vt), suffix)

    @pl.when(c == nch - 1)
    def _():
        lv = lamv_ref[...]
        lam = (jnp.exp(jnp.sum(lv[0:1] * lv[1:2], axis=1, keepdims=True))
               - jnp.exp(jnp.sum(lv[2:3] * lv[3:4], axis=1, keepdims=True))) + lam_init
        o_all = acc_ref[...] / jnp.where(is_c, 1.0, l_ref[:, 0:1])
        lane = _iota((T8, N_HEADS * HEAD_DIM), 1)
        out = jnp.zeros((T8, N_HEADS * HEAD_DIM), F32)
        for hh in range(N_GROUP_HEADS):
            r1, r2 = _slot_rows(SLOT_A1, hh), _slot_rows(SLOT_A2, hh)
            oa = o_all[r1:r1 + T8] - lam * o_all[r2:r2 + T8]
            hm = (lane >= hh * HEAD_DIM) & (lane < (hh + 1) * HEAD_DIM)
            ms = jnp.sum(jnp.where(hm, oa * oa, 0.0), axis=1, keepdims=True) / HEAD_DIM
            oa = oa * lax.rsqrt(ms + LN_EPS) * subg_ref[...] * (1.0 - lam_init)
            out = jnp.where(hm, oa, out)
            for grp, slot in ((1, SLOT_B), (2, SLOT_C), (3, SLOT_D)):
                r = _slot_rows(slot, hh)
                head = grp * N_GROUP_HEADS + hh
                hm = (lane >= head * HEAD_DIM) & (lane < (head + 1) * HEAD_DIM)
                out = jnp.where(hm, o_all[r:r + T8], out)
        o_ref[0] = out.astype(o_ref.dtype)


def _dec_attention(l, lam_init, dec_seq, pt_flat, cache_k, cache_v, qall, k_new, v_new, selp, seln, dsuf,
                   misc_new, lamv, subg16, n_pages):
    db = qall.shape[0]
    n_pg = DEC_HALVES * CP
    nch = n_pages // n_pg
    mix = N_HEADS * HEAD_DIM

    def page_spec(u):
        return pl.BlockSpec((1, 1, mix, PAGE),
                            lambda b, c, pt: (l, pt[b * n_pages + (nch - 1 - c) * n_pg + u], 0, 0))

    per_b = lambda b, c, pt: (b, 0, 0)
    chunk = lambda b, c, pt: (b, nch - 1 - c, 0, 0)
    fixed = lambda b, c, pt: (0, 0)
    return pl.pallas_call(
        functools.partial(_dec_attn_kernel, lam_init, dec_seq),
        grid_spec=pltpu.PrefetchScalarGridSpec(
            num_scalar_prefetch=1,
            grid=(db, nch),
            in_specs=[page_spec(u) for u in range(n_pg)] * 2 + [
                pl.BlockSpec((1, N_QROWS, mix), per_b),
                pl.BlockSpec((1, PAGE, mix), per_b), pl.BlockSpec((1, PAGE, mix), per_b),
                pl.BlockSpec((1, DEC_HALVES, T8, CK), chunk), pl.BlockSpec((1, T8, PAGE), per_b),
                pl.BlockSpec((1, DEC_HALVES, 8, CK), chunk), pl.BlockSpec((1, T8, LANES), per_b),
                pl.BlockSpec((8, LANES), fixed), pl.BlockSpec((1, mix), fixed)],
            out_specs=pl.BlockSpec((1, T8, mix), per_b),
            scratch_shapes=[pltpu.VMEM((N_QROWS, LANES), F32), pltpu.VMEM((N_QROWS, LANES), F32),
                            pltpu.VMEM((_slot_rows(SLOT_D, 0) - _slot_rows(SLOT_C, 0), LANES), F32),
                            pltpu.VMEM((N_QROWS, mix), F32)]),
        out_shape=jax.ShapeDtypeStruct((db, T8, mix), BF16),
        compiler_params=_cparams(("parallel", "arbitrary")),
        name="dec_attn",
    )(pt_flat, *([cache_k] * n_pg), *([cache_v] * n_pg), qall, k_new, v_new, selp, seln, dsuf, misc_new, lamv,
      subg16)


def _slot_lane_masks():
    mix = N_HEADS * HEAD_DIM
    m = np.zeros((5 * N_GROUP_HEADS, mix), np.float32)
    for hh in range(N_GROUP_HEADS):
        m[SLOT_A1 * 4 + hh, hh * HEAD_DIM:hh * HEAD_DIM + DIFF_HALF] = 1
        m[SLOT_A2 * 4 + hh, hh * HEAD_DIM + DIFF_HALF:(hh + 1) * HEAD_DIM] = 1
        for grp, slot in ((1, SLOT_B), (2, SLOT_C), (3, SLOT_D)):
            head = grp * N_GROUP_HEADS + hh
            m[slot * 4 + hh, head * HEAD_DIM:(head + 1) * HEAD_DIM] = 1
    return m


def _sample_attention(l, lam_init, p, rows, lamv, subg2, db, dec_seq):
    qs, kb, vb, qib, misc = rows
    n_s = db * dec_seq
    mix = N_HEADS * HEAD_DIM
    page_table = p["page_table"]
    n_pages = page_table.shape[1]
    pt_flat = page_table.reshape(-1).astype(I32)
    n_keys = n_pages * PAGE + dec_seq
    n_sel = min(DSA_TOPK, n_keys // 4)

    def per_seq(a, pad_to):
        a = a[:n_s].reshape(db, dec_seq, a.shape[-1])
        return jnp.pad(a, ((0, 0), (0, pad_to - dec_seq), (0, 0)))

    misc8 = per_seq(misc, T8)
    qi8 = per_seq(qib, T8).reshape(db, T8, IDX_HEADS, IDX_DIM)
    qim = jnp.swapaxes(qi8, 1, 2).reshape(db, IDX_HEADS * T8, IDX_DIM)
    wv = jnp.swapaxes(misc8[:, :, WI_LANE:WI_LANE + IDX_HEADS], 1, 2).reshape(db, IDX_HEADS * T8, 1)
    wv = jnp.broadcast_to(wv, (db, IDX_HEADS * T8, LANES))
    ki_new = per_seq(misc, PAGE)[:, :, :IDX_DIM]
    selp, seln, dsuf = _dec_index(l, n_sel, dec_seq, pt_flat, p["cache_idx"], p["cache_logf"], qim, wv,
                                  ki_new, n_pages)

    masks = jnp.asarray(_slot_lane_masks(), BF16)
    q8 = per_seq(qs, T8)
    qall = (q8[:, None, :, :] * masks[None, :, None, :]).reshape(db, N_QROWS, mix)
    subg16 = jnp.tile(subg2, (1, mix // LANES))
    o8 = _dec_attention(l, lam_init, dec_seq, pt_flat, p["cache_k"], p["cache_v"], qall,
                        per_seq(kb, PAGE), per_seq(vb, PAGE), selp, seln, dsuf, misc8, lamv, subg16, n_pages)
    return o8[:, :dec_seq].reshape(n_s, mix)


E_LANE = N_EXPERT_GROUPS
BIG_LANE = 4096


def _layer_norm(x, g, b):
    mu = jnp.mean(x, axis=1, keepdims=True)
    xc = x - mu
    var = jnp.mean(xc * xc, axis=1, keepdims=True)
    return xc * lax.rsqrt(var + LN_EPS) * g + b


def _first_lane(mask, lane):
    return jnp.min(jnp.where(mask, lane, BIG_LANE), axis=1, keepdims=True)


def _mix_out_kernel(alpha, oa_ref, ob_ref, oc_ref, od_ref, x_ref, wo_ref, g_ref, b_ref, wr_ref, rb_ref,
                    h_ref, route_ref, cnt_ref, carry_ref):
    t = pl.program_id(0)
    grp_w = N_GROUP_HEADS * HEAD_DIM
    acc = x_ref[...] * alpha
    for n, o_ref in enumerate((oa_ref, ob_ref, oc_ref, od_ref)):
        acc = acc + _dot(o_ref[...], wo_ref[n * grp_w:(n + 1) * grp_w, :])
    h = _layer_norm(acc, g_ref[...], b_ref[...])
    h_ref[...] = h

    h_hi, h_lo = _split2(h)
    lg = (_dot(h_hi, wr_ref[0]) + _dot(h_lo, wr_ref[0]) + _dot(h_hi, wr_ref[1])) + rb_ref[...]
    lane = _iota(lg.shape, 1)
    is_g = lane < E_LANE
    is_e = (lane >= E_LANE) & (lane < E_LANE + N_EXPERTS)
    gmax = jnp.max(jnp.where(is_g, lg, NEG), axis=1, keepdims=True)
    grp = _first_lane(is_g & (lg == gmax), lane)
    pg_top = 1.0 / jnp.sum(jnp.where(is_g, jnp.exp(lg - gmax), 0.0), axis=1, keepdims=True)
    in_grp = is_e & (((lane - E_LANE) // EXPERTS_PER_GROUP) == grp)
    emax = jnp.max(jnp.where(in_grp, lg, NEG), axis=1, keepdims=True)
    ex = jnp.where(in_grp, jnp.exp(lg - emax), 0.0)
    pe = ex / jnp.sum(ex, axis=1, keepdims=True)
    p0 = jnp.max(jnp.where(in_grp, pe, -1.0), axis=1, keepdims=True)
    i0 = _first_lane(in_grp & (pe == p0), lane)
    rest = in_grp & (lane != i0)
    p1 = jnp.max(jnp.where(rest, pe, -1.0), axis=1, keepdims=True)
    i1 = _first_lane(rest & (pe == p1), lane)
    g0 = pg_top * p0 / (p0 + p1)
    g1 = pg_top * p1 / (p0 + p1)

    @pl.when(t == 0)
    def _():
        carry_ref[...] = jnp.zeros_like(carry_ref)
    onehot = jnp.where((lane == i0) | (lane == i1), 1.0, 0.0)
    r = _iota((TM, TM), 0)
    c = _iota((TM, TM), 1)
    before = (c < r).astype(BF16)
    prefix = _dot(before, onehot.astype(BF16)) + carry_ref[0:1, :]
    rank0 = jnp.sum(jnp.where(lane == i0, prefix, 0.0), axis=1, keepdims=True)
    rank1 = jnp.sum(jnp.where(lane == i1, prefix, 0.0), axis=1, keepdims=True)
    total = carry_ref[0:1, :] + jnp.sum(onehot, axis=0, keepdims=True)
    carry_ref[...] = jnp.broadcast_to(total, carry_ref.shape)
    cnt_ref[...] = jnp.broadcast_to(total, cnt_ref.shape)

    vals = (i0.astype(F32) - E_LANE, i1.astype(F32) - E_LANE, g0, g1, rank0, rank1)
    route = jnp.zeros(lg.shape, F32)
    for n, val in enumerate(vals):
        route = jnp.where(lane == n, val, route)
    route_ref[...] = route


def _mix_out(alpha, o4, x, w_out, ln_g, ln_b, wr, rb):
    tp, d = x.shape
    grp_w = N_GROUP_HEADS * HEAD_DIM
    row = lambda t: (t, 0)
    fixed = lambda t: (0, 0)
    return pl.pallas_call(
        functools.partial(_mix_out_kernel, alpha),
        grid=(tp // TM,),
        in_specs=[pl.BlockSpec((TM, grp_w), row)] * 4 + [
            pl.BlockSpec((TM, d), row), pl.BlockSpec(w_out.shape, fixed),
            pl.BlockSpec((1, d), fixed), pl.BlockSpec((1, d), fixed),
            pl.BlockSpec(wr.shape, lambda t: (0, 0, 0)), pl.BlockSpec((1, LANES), fixed)],
        out_specs=(pl.BlockSpec((TM, d), row), pl.BlockSpec((TM, LANES), row), pl.BlockSpec((8, LANES), fixed)),
        out_shape=(jax.ShapeDtypeStruct((tp, d), F32), jax.ShapeDtypeStruct((tp, LANES), F32),
                   jax.ShapeDtypeStruct((8, LANES), F32)),
        scratch_shapes=[pltpu.VMEM((8, LANES), F32)],
        compiler_params=_cparams(("arbitrary",)),
        name="mix_out",
    )(*o4, x, w_out, ln_g, ln_b, wr, rb)


def _dispatch_kernel(dest_ref, h_ref, buf_in_ref, buf_ref, sem):
    del buf_in_ref
    t = pl.program_id(0)

    def copy(r, k):
        d = dest_ref[(t * TM + r) * 2 + k]
        return pltpu.make_async_copy(h_ref.at[pl.ds(r, 1), :], buf_ref.at[pl.ds(d, 1), :], sem)

    def start(r, _):
        copy(r, 0).start()
        copy(r, 1).start()
        return 0

    def wait(r, _):
        copy(r, 0).wait()
        copy(r, 1).wait()
        return 0

    lax.fori_loop(0, TM, start, 0)
    lax.fori_loop(0, TM, wait, 0)


def _dispatch(dest_flat, h, n_rows):
    tp, d = h.shape
    buf0 = jnp.zeros((n_rows, d), F32)
    return pl.pallas_call(
        _dispatch_kernel,
        grid_spec=pltpu.PrefetchScalarGridSpec(
            num_scalar_prefetch=1,
            grid=(tp // TM,),
            in_specs=[pl.BlockSpec((TM, d), lambda t, dest: (t, 0)), pl.BlockSpec(memory_space=pl.ANY)],
            out_specs=pl.BlockSpec(memory_space=pl.ANY),
            scratch_shapes=[pltpu.SemaphoreType.DMA(())]),
        out_shape=jax.ShapeDtypeStruct((n_rows, d), F32),
        input_output_aliases={2: 0},
        compiler_params=_cparams(("arbitrary",)),
        name="moe_dispatch",
    )(dest_flat, h, buf0)


def _expert_kernel(be_ref, nb_ref, x_ref, wgu_ref, wd_ref, o_ref):
    j = pl.program_id(0)

    @pl.when(j < nb_ref[0])
    def _():
        de = wd_ref.shape[1]
        xb = x_ref[...].astype(BF16)
        hid = _dot(xb, wgu_ref[0].astype(BF16))
        a, u = hid[:, :de], hid[:, de:]
        act = (a * (1.0 / (1.0 + jnp.exp(-a)))) * u
        o_ref[...] = _dot(act.astype(BF16), wd_ref[0].astype(BF16))

    @pl.when(j >= nb_ref[0])
    def _():
        o_ref[...] = jnp.zeros_like(o_ref)


def _experts(block_e, n_used, buf, w_gate_up, w_down):
    n_rows, d = buf.shape
    _, _, de2 = w_gate_up.shape
    return pl.pallas_call(
        _expert_kernel,
        grid_spec=pltpu.PrefetchScalarGridSpec(
            num_scalar_prefetch=2,
            grid=(n_rows // MOE_ROWS,),
            in_specs=[pl.BlockSpec((MOE_ROWS, d), lambda j, be, nb: (j, 0)),
                      pl.BlockSpec((1, d, de2), lambda j, be, nb: (be[j], 0, 0)),
                      pl.BlockSpec((1, de2 // 2, d), lambda j, be, nb: (be[j], 0, 0))],
            out_specs=pl.BlockSpec((MOE_ROWS, d), lambda j, be, nb: (j, 0))),
        out_shape=jax.ShapeDtypeStruct((n_rows, d), F32),
        compiler_params=_cparams(("arbitrary",)),
        name="moe_experts",
    )(block_e, n_used, buf, w_gate_up, w_down)


def _combine_kernel(alpha, dest_ref, h_ref, route_ref, g_ref, b_ref, ebuf_ref, y_ref, rows_ref, sem):
    t = pl.program_id(0)

    def copy(r, k):
        d = dest_ref[(t * TM + r) * 2 + k]
        return pltpu.make_async_copy(ebuf_ref.at[pl.ds(d, 1), :], rows_ref.at[k, pl.ds(r, 1), :], sem)

    def start(r, _):
        copy(r, 0).start()
        copy(r, 1).start()
        return 0

    def wait(r, _):
        copy(r, 0).wait()
        copy(r, 1).wait()
        return 0

    lax.fori_loop(0, TM, start, 0)
    lax.fori_loop(0, TM, wait, 0)
    route = route_ref[...]
    f = rows_ref[0] * route[:, 2:3] + rows_ref[1] * route[:, 3:4]
    y_ref[...] = _layer_norm(h_ref[...] * alpha + f, g_ref[...], b_ref[...])


def _combine(alpha, dest_flat, h, route, ln_g, ln_b, ebuf):
    tp, d = h.shape
    row = lambda t, dest: (t, 0)
    fixed = lambda t, dest: (0, 0)
    return pl.pallas_call(
        functools.partial(_combine_kernel, alpha),
        grid_spec=pltpu.PrefetchScalarGridSpec(
            num_scalar_prefetch=1,
            grid=(tp // TM,),
            in_specs=[pl.BlockSpec((TM, d), row), pl.BlockSpec((TM, LANES), row),
                      pl.BlockSpec((1, d), fixed), pl.BlockSpec((1, d), fixed),
                      pl.BlockSpec(memory_space=pl.ANY)],
            out_specs=pl.BlockSpec((TM, d), row),
            scratch_shapes=[pltpu.VMEM((2, TM, d), F32), pltpu.SemaphoreType.DMA(())]),
        out_shape=jax.ShapeDtypeStruct((tp, d), F32),
        compiler_params=_cparams(("arbitrary",)),
        name="moe_combine",
    )(dest_flat, h, route, ln_g, ln_b, ebuf)


def _moe(alpha, h, route, counts, w_gate_up, w_down, ln_g, ln_b):
    tp, _ = h.shape
    eid = route[:, 0:2].astype(I32)
    rank = route[:, 4:6].astype(I32)
    counts = counts.astype(I32)
    padded = (counts + MOE_ROWS - 1) // MOE_ROWS * MOE_ROWS
    pends = jnp.cumsum(padded)
    pstarts = pends - padded
    dest = (pstarts[eid] + rank).reshape(-1)
    n_blocks = -(-(tp * 2) // MOE_ROWS) + N_EXPERTS
    block_start = jnp.arange(n_blocks, dtype=I32) * MOE_ROWS
    block_e = jnp.minimum(jnp.sum(block_start[:, None] >= pends[None, :], axis=-1), N_EXPERTS - 1).astype(I32)
    n_used = (pends[-1:] // MOE_ROWS).astype(I32)
    buf = _dispatch(dest, h, n_blocks * MOE_ROWS)
    ebuf = _experts(block_e, n_used, buf, w_gate_up, w_down)
    return _combine(alpha, dest, h, route, ln_g, ln_b, ebuf)


def _round_up(n, m):
    return -(-n // m) * m


def _concat_rows(x_prompt, x_sample):
    d = x_prompt.shape[-1]
    xp = x_prompt.reshape(-1, d)
    xs = x_sample.reshape(-1, d)
    n = xp.shape[0] + xs.shape[0]
    pad = _round_up(n, TM) - n
    return jnp.concatenate([xp, xs, jnp.zeros((pad, d), xp.dtype)], axis=0)


def _tables(seq, dec_seq, past_len, dec_batch):
    n_s = dec_batch * dec_seq
    pos_s = past_len + (jnp.arange(_round_up(n_s, TM), dtype=I32) % dec_seq)
    return jnp.concatenate([_rope_tables(jnp.arange(seq, dtype=I32)), _rope_tables(pos_s)], axis=0)


def _prep_layer(p, l):
    w_in = p["w_in"][l]
    cols = w_in.shape[1]
    w_pad = jnp.pad(w_in, ((0, 0), (0, _round_up(cols, LANES) - cols))).astype(BF16)
    bf_row = jnp.zeros((1, LANES), F32).at[0, F_LANE:F_LANE + N_GROUP_HEADS].set(p["b_f"][l])
    lamv = jnp.zeros((8, LANES), F32)
    for n, name in enumerate(("lam_q1", "lam_k1", "lam_q2", "lam_k2")):
        lamv = lamv.at[n, :DIFF_HALF].set(p[name][l])
    subg2 = jnp.concatenate([p["sub_g"][l], p["sub_g"][l]])[None, :]
    wr = jnp.zeros((p["router_g"].shape[1], LANES), F32)
    wr = wr.at[:, :E_LANE].set(p["router_g"][l]).at[:, E_LANE:E_LANE + N_EXPERTS].set(p["router_e"][l])
    wr_hi = wr.astype(BF16)
    wr_lo = (wr - wr_hi.astype(F32)).astype(BF16)
    rb = jnp.zeros((1, LANES), F32)
    rb = rb.at[0, :E_LANE].set(p["router_g_b"][l]).at[0, E_LANE:E_LANE + N_EXPERTS].set(p["router_e_b"][l])
    return dict(w_in=w_pad, bf_row=bf_row, lamv=lamv, subg2=subg2, wr=jnp.stack([wr_hi, wr_lo]), rb=rb,
                w_out=p["w_out"][l].astype(BF16))


def kernel(x_prompt, x_sample, cache_k, cache_v, cache_idx, cache_logf, page_table, w_in, b_f, lam_q1, lam_k1,
           lam_q2, lam_k2, sub_g, w_out, ln1_g, ln1_b, router_g, router_g_b, router_e, router_e_b, w_gate_up,
           w_down, ln2_g, ln2_b):
    p = dict(cache_k=cache_k, cache_v=cache_v, cache_idx=cache_idx, cache_logf=cache_logf,
             page_table=page_table, w_in=w_in, b_f=b_f, lam_q1=lam_q1, lam_k1=lam_k1, lam_q2=lam_q2,
             lam_k2=lam_k2, sub_g=sub_g, w_out=w_out, router_g=router_g, router_g_b=router_g_b,
             router_e=router_e, router_e_b=router_e_b)
    depth = w_in.shape[0]
    n_seq, seq, d = x_prompt.shape
    db, dec_seq, _ = x_sample.shape
    t, n_s = n_seq * seq, db * dec_seq
    assert seq % TQ == 0 and t % TM == 0 and n_s <= TM and dec_seq <= T8
    assert page_table.shape[1] % (CP * DEC_HALVES) == 0 and cache_k.shape[2] == PAGE
    alpha = (2 * depth) ** 0.25
    past_len = page_table.shape[1] * PAGE
    tabs = _tables(seq, dec_seq, past_len, db)
    mix = N_HEADS * HEAD_DIM
    p["cache_k"] = jnp.transpose(cache_k, (0, 1, 3, 4, 2)).reshape(cache_k.shape[0], cache_k.shape[1], mix, PAGE)
    p["cache_v"] = jnp.transpose(cache_v, (0, 1, 3, 4, 2)).reshape(cache_v.shape[0], cache_v.shape[1], mix, PAGE)
    p["cache_idx"] = jnp.swapaxes(cache_idx, 2, 3)
    p["cache_logf"] = jnp.pad(jnp.swapaxes(cache_logf, 2, 3), ((0, 0), (0, 0), (0, 8 - N_GROUP_HEADS), (0, 0)))

    y = _concat_rows(x_prompt, x_sample)
    tp = y.shape[0]
    rows_p, rows_s = [], []
    mix = N_HEADS * HEAD_DIM
    n_pt, n_st = t // TM, (tp - t) // TM
    tiles_per_seq = seq // TM
    kv_p = (jnp.zeros((depth, t, mix), F32), jnp.zeros((depth, t, mix), F32))
    kv_s = (jnp.zeros((depth, tp - t, mix), F32), jnp.zeros((depth, tp - t, mix), F32))
    for l in range(depth):
        lam_init = 0.8 - 0.6 * math.exp(-0.3 * l)
        st = _prep_layer(p, l)
        qs, kb, _, kf, vf, qib, misc, ki2b, vx = _project(
            y, st["w_in"], tabs, st["bf_row"], 0, n_pt, lambda i: (i % tiles_per_seq, 0), l, depth, kv_p)
        kv_p = (kf, vf)
        qs_s, kb_s, vb_s, kf_s, vf_s, qib_s, misc_s, _, _ = _project(
            y, st["w_in"], tabs, st["bf_row"], n_pt, n_st, lambda i: (tiles_per_seq + i, 0), l, depth, kv_s)
        kv_s = (kf_s, vf_s)
        fcol, ft = _fcum(misc, n_seq, seq)
        o_p = _prompt_attention(lam_init, qs, kb, vx, qib, ki2b, misc, fcol, ft, st["lamv"], st["subg2"],
                                n_seq, seq)
        o_s = _sample_attention(l, lam_init, p, (qs_s, kb_s, vb_s, qib_s, misc_s),
                                st["lamv"], st["subg2"], db, dec_seq)
        grp_w = N_GROUP_HEADS * HEAD_DIM
        o4 = [jnp.concatenate([o_p[g], o_s[:, g * grp_w:(g + 1) * grp_w],
                               jnp.zeros((tp - t - n_s, grp_w), BF16)], axis=0) for g in range(4)]
        h, route, cnt = _mix_out(alpha, o4, y, st["w_out"], ln1_g[l][None, :], ln1_b[l][None, :],
                                 st["wr"], st["rb"])
        y = _moe(alpha, h, route, cnt[0, E_LANE:E_LANE + N_EXPERTS], w_gate_up[l], w_down[l],
                 ln2_g[l][None, :], ln2_b[l][None, :])
        rows_p.append((misc[:, :IDX_DIM], misc[:, F_LANE:F_LANE + N_GROUP_HEADS]))
        rows_s.append((misc_s[:n_s, :IDX_DIM], misc_s[:n_s, F_LANE:F_LANE + N_GROUP_HEADS]))

    def stack(rows, n, lead, tail):
        return jnp.stack([r[n].reshape(*lead, *tail) for r in rows])

    outs = [y[:t].reshape(n_seq, seq, d), y[t:t + n_s].reshape(db, dec_seq, d)]
    outs += [a.reshape(depth, n_seq, seq, N_HEADS, HEAD_DIM) for a in kv_p]
    outs += [stack(rows_p, 0, (n_seq, seq), (IDX_DIM,)), stack(rows_p, 1, (n_seq, seq), (N_GROUP_HEADS,))]
    outs += [a[:, :n_s].reshape(depth, db, dec_seq, N_HEADS, HEAD_DIM) for a in kv_s]
    outs += [stack(rows_s, 0, (db, dec_seq), (IDX_DIM,)), stack(rows_s, 1, (db, dec_seq), (N_GROUP_HEADS,))]
    return tuple(outs)
```

```python
import functools
import math

import jax
import jax.numpy as jnp
import numpy as np
from jax import lax
from jax.experimental import pallas as pl
from jax.experimental.pallas import tpu as pltpu

F32 = jnp.float32
BF16 = jnp.bfloat16
I32 = jnp.int32

HEAD_DIM = 64
N_HEADS = 16
N_GROUP_HEADS = 4
DIFF_HALF = 32
IDX_HEADS = 8
IDX_DIM = 64
DSA_TOPK = 256
ROPE_THETA = 10000.0
N_EXPERT_GROUPS = 4
EXPERTS_PER_GROUP = 8
N_EXPERTS = 32
LN_EPS = 1e-5

LANES = 128
VMEM_LIMIT = 56 * 1024 * 1024

TM = 256
TQ = 256
MOE_ROWS = 256
NEG = -1e30
INT_MIN = -2 ** 31

WI_LANE = IDX_DIM
F_LANE = IDX_DIM + IDX_HEADS


def _cparams(sem):
    return pltpu.CompilerParams(dimension_semantics=sem, vmem_limit_bytes=VMEM_LIMIT)


def _dot(a, b):
    return jnp.dot(a, b, preferred_element_type=F32)


def _dot_nt(a, b):
    return lax.dot_general(a, b, (((1,), (1,)), ((), ())), preferred_element_type=F32)


def _split2(x):
    hi = x.astype(BF16)
    lo = (x - hi.astype(F32)).astype(BF16)
    return hi, lo


def _split3(x):
    hi = x.astype(BF16)
    r = x - hi.astype(F32)
    mid = r.astype(BF16)
    lo = (r - mid.astype(F32)).astype(BF16)
    return hi, mid, lo


def _log_sigmoid(z):
    return jnp.minimum(z, 0.0) - jnp.log1p(jnp.exp(-jnp.abs(z)))


def _iota(shape, dim):
    return lax.broadcasted_iota(I32, shape, dim)


def _rope_chunk(y, cos, sin_signed, half):
    lane = _iota(y.shape, 1)
    first = (lane % (2 * half)) < half
    fwd = pltpu.roll(y, LANES - half, 1)
    bwd = pltpu.roll(y, half, 1)
    return y * cos + jnp.where(first, fwd, bwd) * sin_signed


def _proj_kernel(x_ref, w_ref, tab_ref, bf_ref, *refs):
    qs_ref, kb_ref, vb_ref, kf_ref, vf_ref, qib_ref, misc_ref, ki2_ref, vx_ref = refs[-9:]
    xb = x_ref[...].astype(BF16)
    cos_d, sin_d = tab_ref[:, 0:128], tab_ref[:, 128:256]
    cos_f, sin_f = tab_ref[:, 256:384], tab_ref[:, 384:512]
    mix = N_HEADS * HEAD_DIM
    wide = 2 * LANES

    def chunk2(c2):
        y = _dot(xb, w_ref[:, c2 * wide:(c2 + 1) * wide])
        return y[:, :LANES], y[:, LANES:]

    def roped(c, y):
        if c < 2:
            return _rope_chunk(y, cos_d, sin_d, DIFF_HALF // 2)
        if c < 4:
            return _rope_chunk(y, cos_f, sin_f, HEAD_DIM // 2)
        return y

    nch = mix // LANES
    n2 = nch // 2
    for c2 in range(n2):
        q2, k2, v2 = chunk2(c2), chunk2(n2 + c2), chunk2(2 * n2 + c2)
        for half in range(2):
            c = 2 * c2 + half
            q = roped(c, q2[half])
            scale = DIFF_HALF ** -0.5 if c < 2 else HEAD_DIM ** -0.5
            qs_ref[:, c * LANES:(c + 1) * LANES] = (q * scale).astype(BF16)
            k = roped(c, k2[half])
            kf_ref[0, :, c * LANES:(c + 1) * LANES] = k
            kb_ref[:, c * LANES:(c + 1) * LANES] = k.astype(BF16)
            v = v2[half]
            vf_ref[0, :, c * LANES:(c + 1) * LANES] = v
            vb_ref[:, c * LANES:(c + 1) * LANES] = v.astype(BF16)
            low = _iota(v.shape, 1) < HEAD_DIM
            vx_ref[:, 2 * c * LANES:(2 * c + 1) * LANES] = jnp.where(low, v, 1.0).astype(BF16)
            vx_ref[:, (2 * c + 1) * LANES:(2 * c + 2) * LANES] = jnp.where(
                low, pltpu.roll(v, HEAD_DIM, 1), 1.0).astype(BF16)
    for c2 in range(IDX_HEADS * IDX_DIM // wide):
        for half, y in enumerate(chunk2(3 * n2 + c2)):
            c = 2 * c2 + half
            qi = _rope_chunk(y, cos_f, sin_f, IDX_DIM // 2)
            qib_ref[:, c * LANES:(c + 1) * LANES] = qi.astype(BF16)
    misc_col = 3 * mix + IDX_HEADS * IDX_DIM
    raw = _dot(xb, w_ref[:, misc_col:misc_col + LANES])
    ki = _rope_chunk(raw, cos_f, sin_f, IDX_DIM // 2)
    lane = _iota(raw.shape, 1)
    wi_scale = IDX_DIM ** -0.5 * IDX_HEADS ** -0.5
    logf = _log_sigmoid(raw + bf_ref[...])
    misc = jnp.where(lane < IDX_DIM, ki,
                     jnp.where(lane < F_LANE, raw * wi_scale,
                               jnp.where(lane < F_LANE + N_GROUP_HEADS, logf, 0.0)))
    misc_ref[...] = misc
    ki2_ref[...] = jnp.where(lane < IDX_DIM, ki, pltpu.roll(ki, IDX_DIM, 1)).astype(BF16)


def _project(x, w_pad, tab, bf_row, tile0, n_tiles, tab_map, layer, depth, kv_prev=None):
    _, d = x.shape
    mix = N_HEADS * HEAD_DIM
    n = n_tiles * TM
    row = lambda t: (t, 0)
    fixed = lambda t: (0, 0)
    slab = pl.BlockSpec((1, TM, mix), lambda t: (layer, t, 0))
    out_shapes = (
        jax.ShapeDtypeStruct((n, mix), BF16), jax.ShapeDtypeStruct((n, mix), BF16),
        jax.ShapeDtypeStruct((n, mix), BF16), jax.ShapeDtypeStruct((depth, n, mix), F32),
        jax.ShapeDtypeStruct((depth, n, mix), F32), jax.ShapeDtypeStruct((n, IDX_HEADS * IDX_DIM), BF16),
        jax.ShapeDtypeStruct((n, LANES), F32), jax.ShapeDtypeStruct((n, LANES), BF16),
        jax.ShapeDtypeStruct((n, 2 * mix), BF16))
    in_specs = [pl.BlockSpec((TM, d), lambda t: (tile0 + t, 0)), pl.BlockSpec(w_pad.shape, fixed),
                pl.BlockSpec((TM, 512), tab_map), pl.BlockSpec((1, LANES), fixed)]
    args = [x, w_pad, tab, bf_row]
    aliases = {}
    if kv_prev is not None:
        in_specs += [pl.BlockSpec(memory_space=pl.ANY)] * 2
        args += list(kv_prev)
        aliases = {4: 3, 5: 4}
    return pl.pallas_call(
        _proj_kernel,
        grid=(n_tiles,),
        in_specs=in_specs,
        out_specs=(pl.BlockSpec((TM, mix), row), pl.BlockSpec((TM, mix), row), pl.BlockSpec((TM, mix), row),
                   slab, slab,
                   pl.BlockSpec((TM, IDX_HEADS * IDX_DIM), row), pl.BlockSpec((TM, LANES), row),
                   pl.BlockSpec((TM, LANES), row), pl.BlockSpec((TM, 2 * mix), row)),
        out_shape=out_shapes,
        input_output_aliases=aliases,
        compiler_params=_cparams(("parallel",)),
        name="proj",
    )(*args)


def _rope_tables(pos):
    def pattern(half):
        lane = jnp.arange(LANES)
        inv = ROPE_THETA ** (-(lane % half).astype(F32) / half)
        ang = pos.astype(F32)[:, None] * inv[None, :]
        sign = jnp.where((lane % (2 * half)) < half, -1.0, 1.0)
        return jnp.cos(ang), jnp.sin(ang) * sign[None, :]
    cd, sd = pattern(DIFF_HALF // 2)
    cf, sf = pattern(HEAD_DIM // 2)
    return jnp.concatenate([cd, sd, cf, sf], axis=1)


def _fcum_kernel(misc_ref, fcol_ref, ft_ref):
    s = misc_ref.shape[0]
    r = _iota((TQ, TQ), 0)
    c = _iota((TQ, TQ), 1)
    tri = (r >= c).astype(BF16)
    sel = (_iota((8, LANES), 1) == _iota((8, LANES), 0) + F_LANE).astype(BF16)
    carry = jnp.zeros((1, LANES), F32)
    for b in range(s // TQ):
        seg = misc_ref[b * TQ:(b + 1) * TQ, :]
        cs = carry
        for piece in _split3(seg):
            cs = cs + _dot(tri, piece)
        fcol_ref[b * TQ:(b + 1) * TQ, :] = cs
        tr = jnp.zeros((8, TQ), F32)
        for piece in _split3(cs):
            tr = tr + _dot_nt(sel, piece)
        ft_ref[b] = tr
        carry = cs[TQ - 1:TQ, :]


def _fcum(misc, n_seq, seq):
    nkb = seq // TQ
    return pl.pallas_call(
        _fcum_kernel,
        grid=(n_seq,),
        in_specs=[pl.BlockSpec((seq, LANES), lambda b: (b, 0))],
        out_specs=(pl.BlockSpec((seq, LANES), lambda b: (b, 0)),
                   pl.BlockSpec((nkb, 8, TQ), lambda b: (b, 0, 0))),
        out_shape=(jax.ShapeDtypeStruct((n_seq * seq, LANES), F32),
                   jax.ShapeDtypeStruct((n_seq * nkb, 8, TQ), F32)),
        compiler_params=_cparams(("parallel",)),
        name="fcum",
    )(misc)


def _half_mask(width, offset, size):
    lane = _iota((1, LANES), 1)
    return (lane >= offset) & (lane < offset + size)


def _masked(q, offset, size):
    return jnp.where(_half_mask(LANES, offset, size), q, jnp.zeros_like(q))


def _kblock(ref, j, width, col):
    return ref[pl.ds(pl.multiple_of(j * TQ, TQ), TQ), col * width:(col + 1) * width]


def _head_q(q_ref, hh, offset=0, size=HEAD_DIM):
    qp = q_ref[:, (hh // 2) * LANES:(hh // 2 + 1) * LANES]
    return _masked(qp, (hh % 2) * HEAD_DIM + offset, size)


def _diag_mask(strict=False):
    row = _iota((TQ, TQ), 0)
    col = _iota((TQ, TQ), 1)
    return (col < row) if strict else (col <= row)


def _softmax_reset(m_ref, acc_ref):
    m_ref[...] = jnp.full(m_ref.shape, NEG, F32)
    acc_ref[...] = jnp.zeros_like(acc_ref)


def _wide(x):
    return jnp.concatenate([x] * (TQ // LANES), axis=1)


def _row_bcast(col):
    return jnp.broadcast_to(col, (TQ, LANES))


def _softmax_block(n, s, vx, m_ref, acc_ref):
    m_old = m_ref[n]
    m_new = jnp.maximum(m_old, _row_bcast(jnp.max(s, axis=1, keepdims=True)))
    alpha = jnp.exp(m_old - m_new)
    p = jnp.exp(s - _wide(m_new))
    acc_ref[n] = alpha * acc_ref[n] + _dot(p.astype(BF16), vx)
    m_ref[n] = m_new


def _softmax_result(n, acc_ref):
    acc = acc_ref[n]
    return acc / pltpu.roll(acc, HEAD_DIM, 1)


def _store_heads(o_ref, outs):
    lane = _iota((TQ, LANES), 1)
    for pair in range(2):
        o = jnp.where(lane < HEAD_DIM, outs[2 * pair], pltpu.roll(outs[2 * pair + 1], HEAD_DIM, 1))
        o_ref[:, pair * LANES:(pair + 1) * LANES] = o.astype(o_ref.dtype)


def _attn_diff_kernel(lam_init, q_ref, k_ref, vx_ref, lamv_ref, subg_ref, o_ref, m_ref, acc_ref):
    i = pl.program_id(1)
    lv = lamv_ref[...]
    lam = (jnp.exp(jnp.sum(lv[0:1] * lv[1:2], axis=1, keepdims=True))
           - jnp.exp(jnp.sum(lv[2:3] * lv[3:4], axis=1, keepdims=True))) + lam_init
    _softmax_reset(m_ref, acc_ref)
    qs = [_head_q(q_ref, hh, part * DIFF_HALF, DIFF_HALF) for hh in range(N_GROUP_HEADS) for part in range(2)]

    def block(j, mask):
        for hh in range(N_GROUP_HEADS):
            kp = _kblock(k_ref, j, LANES, hh // 2)
            vx = _kblock(vx_ref, j, LANES, hh)
            for part in range(2):
                s = _dot_nt(qs[2 * hh + part], kp)
                if mask is not None:
                    s = jnp.where(mask, s, NEG)
                _softmax_block(2 * hh + part, s, vx, m_ref, acc_ref)

    def body(j, carry):
        block(j, None)
        return carry
    lax.fori_loop(0, i, body, 0)
    block(i, _diag_mask())

    outs = []
    lane = _iota((TQ, LANES), 1)
    for hh in range(N_GROUP_HEADS):
        oa = _softmax_result(2 * hh, acc_ref) - lam * _softmax_result(2 * hh + 1, acc_ref)
        ms = jnp.sum(jnp.where(lane < HEAD_DIM, oa * oa, 0.0), axis=1, keepdims=True) / HEAD_DIM
        outs.append(oa * lax.rsqrt(ms + LN_EPS) * subg_ref[...] * (1.0 - lam_init))
    _store_heads(o_ref, outs)


def _attn_fox_kernel(q_ref, k_ref, vx_ref, fcol_ref, ft_ref, o_ref, m_ref, acc_ref):
    i = pl.program_id(1)
    _softmax_reset(m_ref, acc_ref)
    qs = [_head_q(q_ref, hh) for hh in range(N_GROUP_HEADS)]
    fqs = [_row_bcast(fcol_ref[:, F_LANE + g:F_LANE + g + 1]) for g in range(N_GROUP_HEADS)]

    def block(j, mask):
        ft = ft_ref[j]
        for hh in range(N_GROUP_HEADS):
            s = _dot_nt(qs[hh], _kblock(k_ref, j, LANES, hh // 2)) + (_wide(fqs[hh]) - ft[hh:hh + 1, :])
            if mask is not None:
                s = jnp.where(mask, s, NEG)
            _softmax_block(hh, s, _kblock(vx_ref, j, LANES, hh), m_ref, acc_ref)

    def body(j, carry):
        block(j, None)
        return carry
    lax.fori_loop(0, i, body, 0)
    block(i, _diag_mask())
    _store_heads(o_ref, [_softmax_result(hh, acc_ref) for hh in range(N_GROUP_HEADS)])


def _attn_sb_kernel(q_ref, k_ref, vx_ref, o_ref, c_ref, acc_ref):
    i = pl.program_id(1)
    suffix = (_iota((TQ, TQ), 0) >= _iota((TQ, TQ), 1)).astype(BF16)
    c_ref[...] = jnp.zeros_like(c_ref)
    acc_ref[...] = jnp.zeros_like(acc_ref)
    qs = [_head_q(q_ref, hh) for hh in range(N_GROUP_HEADS)]

    def block(j, mask):
        for hh in range(N_GROUP_HEADS):
            z = _dot_nt(qs[hh], _kblock(k_ref, j, LANES, hh // 2))
            ls = jnp.minimum(z, 0.0) - jnp.log(1.0 + jnp.exp(-jnp.abs(z)))
            l1m = ls - z
            if mask is not None:
                l1m = jnp.where(mask, l1m, 0.0)
            hi, lo = _split2(l1m)
            incl = _dot(hi, suffix) + _dot(lo, suffix)
            carry = c_ref[hh]
            w = jnp.exp(ls + (incl - l1m + _wide(carry)))
            if mask is not None:
                w = jnp.where(mask, w, 0.0)
            acc_ref[hh] = acc_ref[hh] + _dot(w.astype(BF16), _kblock(vx_ref, j, LANES, hh))
            c_ref[hh] = carry + _row_bcast(incl[:, 0:1])

    block(i, _diag_mask(strict=True))

    def body(jj, carry):
        block(i - 1 - jj, None)
        return carry
    lax.fori_loop(0, i, body, 0)
    _store_heads(o_ref, [acc_ref[hh] for hh in range(N_GROUP_HEADS)])


def _sort_key(x):
    bits = pltpu.bitcast(x, I32)
    return bits ^ ((bits >> 31) & jnp.int32(0x7FFFFFFF))


def _topk_threshold(count_ge, n_sel, shape, steps=32):
    def body(b, tau):
        inc = lax.shift_left(jnp.int32(1), jnp.int32(31) - b)
        cand = tau + inc
        return jnp.where(count_ge(cand) >= n_sel, cand, tau)
    return lax.fori_loop(0, steps, body, jnp.full(shape, INT_MIN, I32))


def _attn_dsa_kernel(n_sel, q_ref, k_ref, vx_ref, qi_ref, ki2_ref, misc_ref, o_ref, key_ref, sel_ref,
                     m_ref, acc_ref):
    i = pl.program_id(1)
    nb = i + 1

    qis = [_masked(qi_ref[:, (ih // 2) * LANES:(ih // 2 + 1) * LANES], (ih % 2) * IDX_DIM, IDX_DIM)
           for ih in range(IDX_HEADS)]
    ws = [misc_ref[:, WI_LANE + ih:WI_LANE + ih + 1] for ih in range(IDX_HEADS)]

    def scores(j):
        kip = _kblock(ki2_ref, j, LANES, 0)
        isc = jnp.zeros((TQ, TQ), F32)
        for ih in range(IDX_HEADS):
            isc = isc + jnp.maximum(_dot_nt(qis[ih], kip), 0.0) * ws[ih]
        return _sort_key(isc)

    def score_block(j, carry):
        key_ref[j] = scores(j)
        return carry
    lax.fori_loop(0, i, score_block, 0)
    key_ref[i] = jnp.where(_diag_mask(), scores(i), INT_MIN)

    def count(pred):
        def body(j, acc):
            key = key_ref[j]
            for c in range(TQ // LANES):
                acc = acc + jnp.where(pred(key[:, c * LANES:(c + 1) * LANES]), 1.0, 0.0)
            return acc
        acc = lax.fori_loop(0, nb, body, jnp.zeros((TQ, LANES), F32))
        return _row_bcast(jnp.sum(acc, axis=1, keepdims=True))

    steps = jnp.where((i == 0) & (n_sel >= TQ), 0, 32)
    tau = _topk_threshold(lambda cand: count(lambda k: k >= cand), float(n_sel), (TQ, LANES), steps)
    need = _wide(float(n_sel) - count(lambda k: k > tau))
    tau = _wide(tau)

    before = (_iota((TQ, TQ), 0) < _iota((TQ, TQ), 1)).astype(BF16)
    ones = jnp.ones((TQ, LANES), BF16)

    def select_block(j, seen):
        key = key_ref[j]
        eq = (key == tau) & (key != INT_MIN)
        eqb = jnp.where(eq, 1.0, 0.0).astype(BF16)
        prefix = _dot(eqb, before) + _wide(seen)
        sel = (key > tau) | (eq & (prefix < need))
        sel_ref[j] = jnp.where(sel, 0.0, NEG)
        return seen + _dot(eqb, ones)
    lax.fori_loop(0, nb, select_block, jnp.zeros((TQ, LANES), F32))

    _softmax_reset(m_ref, acc_ref)
    qs = [_head_q(q_ref, hh) for hh in range(N_GROUP_HEADS)]

    def body(j, carry):
        bias = sel_ref[j]
        for hh in range(N_GROUP_HEADS):
            s = _dot_nt(qs[hh], _kblock(k_ref, j, LANES, hh // 2)) + bias
            _softmax_block(hh, s, _kblock(vx_ref, j, LANES, hh), m_ref, acc_ref)
        return carry
    lax.fori_loop(0, nb, body, 0)
    _store_heads(o_ref, [_softmax_result(hh, acc_ref) for hh in range(N_GROUP_HEADS)])


def _prompt_attention(lam_init, qs, kb, vx, qib, ki2b, misc, fcol, ft, lamv, subg2, n_seq, seq):
    nq = seq // TQ
    nkb = seq // TQ
    t = n_seq * seq
    grp = N_GROUP_HEADS * HEAD_DIM

    def qspec(g):
        return pl.BlockSpec((TQ, grp), lambda b, i: (b * nq + i, g))

    def kspec(g):
        return pl.BlockSpec((seq, grp), lambda b, i: (b, g))

    def vspec(g):
        return pl.BlockSpec((seq, 2 * grp), lambda b, i: (b, g))

    ospec = pl.BlockSpec((TQ, grp), lambda b, i: (b * nq + i, 0))
    oshape = jax.ShapeDtypeStruct((t, grp), BF16)
    fixed = lambda b, i: (0, 0)
    rowblk = lambda b, i: (b * nq + i, 0)
    params = _cparams(("parallel", "arbitrary"))

    def state(n):
        return [pltpu.VMEM((n, TQ, LANES), F32), pltpu.VMEM((n, TQ, LANES), F32)]

    o_a = pl.pallas_call(
        functools.partial(_attn_diff_kernel, lam_init), grid=(n_seq, nq),
        in_specs=[qspec(0), kspec(0), vspec(0), pl.BlockSpec((8, LANES), fixed),
                  pl.BlockSpec((1, LANES), fixed)],
        out_specs=ospec, out_shape=oshape, scratch_shapes=state(2 * N_GROUP_HEADS),
        compiler_params=params, name="attn_diff",
    )(qs, kb, vx, lamv, subg2)
    n_sel = min(DSA_TOPK, seq // 4)
    o_b = pl.pallas_call(
        functools.partial(_attn_dsa_kernel, n_sel), grid=(n_seq, nq),
        in_specs=[qspec(1), kspec(1), vspec(1),
                  pl.BlockSpec((TQ, IDX_HEADS * IDX_DIM), rowblk),
                  pl.BlockSpec((seq, LANES), lambda b, i: (b, 0)),
                  pl.BlockSpec((TQ, LANES), rowblk)],
        out_specs=ospec, out_shape=oshape,
        scratch_shapes=[pltpu.VMEM((nkb, TQ, TQ), I32), pltpu.VMEM((nkb, TQ, TQ), F32)] + state(N_GROUP_HEADS),
        compiler_params=params, name="attn_dsa",
    )(qs, kb, vx, qib, ki2b, misc)
    o_c = pl.pallas_call(
        _attn_sb_kernel, grid=(n_seq, nq),
        in_specs=[qspec(2), kspec(2), vspec(2)],
        out_specs=ospec, out_shape=oshape, scratch_shapes=state(N_GROUP_HEADS),
        compiler_params=params, name="attn_sb",
    )(qs, kb, vx)
    o_d = pl.pallas_call(
        _attn_fox_kernel, grid=(n_seq, nq),
        in_specs=[qspec(3), kspec(3), vspec(3), pl.BlockSpec((TQ, LANES), rowblk),
                  pl.BlockSpec((nkb, 8, TQ), lambda b, i: (b, 0, 0))],
        out_specs=ospec, out_shape=oshape, scratch_shapes=state(N_GROUP_HEADS),
        compiler_params=params, name="attn_fox",
    )(qs, kb, vx, fcol, ft)
    return o_a, o_b, o_c, o_d


PAGE = 128
CP = 4
CK = CP * PAGE
DEC_HALVES = 1
T8 = 8
SLOT_A1, SLOT_A2, SLOT_B, SLOT_C, SLOT_D = range(5)
N_QROWS = 5 * N_GROUP_HEADS * T8


def _dec_index_kernel(n_sel, dec_seq, pt_ref, *refs):
    ki_refs = refs[:CP]
    lf_refs = refs[CP:2 * CP]
    qi_ref, w_ref, kin_ref, selp_ref, seln_ref, dsuf_ref, key_ref, carry_ref = refs[2 * CP:]
    c = pl.program_id(1)
    nch = pl.num_programs(1)

    def scores(s):
        s = jnp.maximum(s, 0.0) * w_ref[0][:, 0:1]
        isc = jnp.zeros((T8, s.shape[1]), F32)
        for ih in range(IDX_HEADS):
            isc = isc + s[ih * T8:(ih + 1) * T8, :]
        return isc

    kit = jnp.concatenate([r[0, 0] for r in ki_refs], axis=1).astype(BF16)
    key_ref[nch - 1 - c] = _sort_key(scores(_dot(qi_ref[0], kit)))

    @pl.when(c == 0)
    def _():
        carry_ref[...] = jnp.zeros_like(carry_ref)
    lf = jnp.concatenate([r[0, 0] for r in lf_refs], axis=1)
    after = (_iota((CK, CK), 0) > _iota((CK, CK), 1)).astype(BF16)
    carry = carry_ref[:, 0:1]
    suf = carry
    for piece in _split3(lf):
        suf = suf + _dot(piece, after)
    dsuf_ref[0, 0] = suf
    carry_ref[...] = jnp.broadcast_to(carry + jnp.sum(lf, axis=1, keepdims=True), carry_ref.shape)

    @pl.when(c == nch - 1)
    def _():
        t8 = _iota((T8, PAGE), 0)
        j = _iota((T8, PAGE), 1)
        keyn = jnp.where((j <= t8) & (j < dec_seq),
                         _sort_key(scores(_dot_nt(qi_ref[0], kin_ref[0].astype(BF16)))), INT_MIN)
        n_blk = key_ref.shape[0]

        def count(pred):
            def body(b, acc):
                return acc + jnp.where(pred(key_ref[b]), 1.0, 0.0)
            acc = lax.fori_loop(0, n_blk, body, jnp.zeros((T8, CK), F32))
            return (jnp.sum(acc, axis=1, keepdims=True)
                    + jnp.sum(jnp.where(pred(keyn), 1.0, 0.0), axis=1, keepdims=True))

        tau = _topk_threshold(lambda cand: count(lambda k: k >= cand), float(n_sel), (T8, 1))
        need = float(n_sel) - count(lambda k: k > tau)
        before = (_iota((CK, CK), 0) < _iota((CK, CK), 1)).astype(BF16)

        def select(key, seen, tri):
            eq = (key == tau) & (key != INT_MIN)
            eqf = jnp.where(eq, 1.0, 0.0)
            prefix = _dot(eqf.astype(BF16), tri) + seen
            sel = (key > tau) | (eq & (prefix < need))
            return jnp.where(sel, 0.0, NEG), seen + jnp.sum(eqf, axis=1, keepdims=True)

        def body(b, seen):
            bias, seen = select(key_ref[b], seen, before)
            selp_ref[0, b] = bias
            return seen
        seen = lax.fori_loop(0, n_blk, body, jnp.zeros((T8, 1), F32))
        bias, _ = select(keyn, seen, before[:PAGE, :PAGE])
        seln_ref[0] = bias


def _dec_index(l, n_sel, dec_seq, pt_flat, idx_t, logf_t, qim, wv, ki_new, n_pages):
    db = qim.shape[0]
    nch = n_pages // CP

    def page_spec(u, rows):
        return pl.BlockSpec((1, 1, rows, PAGE),
                            lambda b, c, pt: (l, pt[b * n_pages + (nch - 1 - c) * CP + u], 0, 0))

    per_b = lambda b, c, pt: (b, 0, 0)
    chunk = lambda b, c, pt: (b, nch - 1 - c, 0, 0)
    return pl.pallas_call(
        functools.partial(_dec_index_kernel, n_sel, dec_seq),
        grid_spec=pltpu.PrefetchScalarGridSpec(
            num_scalar_prefetch=1,
            grid=(db, nch),
            in_specs=[page_spec(u, IDX_DIM) for u in range(CP)] + [page_spec(u, 8) for u in range(CP)] + [
                pl.BlockSpec((1, IDX_HEADS * T8, IDX_DIM), per_b),
                pl.BlockSpec((1, IDX_HEADS * T8, LANES), per_b),
                pl.BlockSpec((1, PAGE, IDX_DIM), per_b)],
            out_specs=(pl.BlockSpec((1, nch, T8, CK), lambda b, c, pt: (b, 0, 0, 0)),
                       pl.BlockSpec((1, T8, PAGE), per_b),
                       pl.BlockSpec((1, 1, 8, CK), chunk)),
            scratch_shapes=[pltpu.VMEM((nch, T8, CK), I32), pltpu.VMEM((8, LANES), F32)]),
        out_shape=(jax.ShapeDtypeStruct((db, nch, T8, CK), F32), jax.ShapeDtypeStruct((db, T8, PAGE), F32),
                   jax.ShapeDtypeStruct((db, nch, 8, CK), F32)),
        compiler_params=_cparams(("parallel", "arbitrary")),
        name="dec_index",
    )(pt_flat, *([idx_t] * CP), *([logf_t] * CP), qim, wv, ki_new)


def _slot_rows(slot, hh):
    return (slot * N_GROUP_HEADS + hh) * T8


def _dec_attn_kernel(lam_init, dec_seq, pt_ref, *refs):
    n_pg = DEC_HALVES * CP
    k_refs = refs[:n_pg]
    v_refs = refs[n_pg:2 * n_pg]
    (q_ref, kn_ref, vn_ref, selp_ref, seln_ref, dsuf_ref, miscn_ref, lamv_ref, subg_ref,
     o_ref, m_ref, l_ref, cc_ref, acc_ref) = refs[2 * n_pg:]
    c = pl.program_id(1)
    nch = pl.num_programs(1)
    rows = _iota((N_QROWS, 1), 0)
    is_c = (rows >= _slot_rows(SLOT_C, 0)) & (rows < _slot_rows(SLOT_D, 0))
    c0, c1 = _slot_rows(SLOT_C, 0), _slot_rows(SLOT_D, 0)
    b0 = _slot_rows(SLOT_B, 0)
    q = q_ref[0]

    lfn = miscn_ref[0]
    t8 = _iota((T8, LANES), 0)
    cn = jnp.zeros((T8, LANES), F32)
    for j in range(dec_seq):
        cn = cn + jnp.where(t8 >= j, lfn[j:j + 1, :], 0.0)

    def update(s, bias, ok_c, pv, suffix):
        m_old, l_old, carry = m_ref[:, 0:1], l_ref[:, 0:1], cc_ref[:, 0:1]
        sm = s + bias
        m_new = jnp.maximum(m_old, jnp.max(sm, axis=1, keepdims=True))
        alpha = jnp.exp(m_old - m_new)
        p = jnp.exp(sm - m_new)
        l_new = alpha * l_old + jnp.sum(p, axis=1, keepdims=True)
        z = s[c0:c1]
        ls = _log_sigmoid(z)
        l1m = jnp.where(ok_c, ls - z, 0.0)
        hi, lo = _split2(l1m)
        incl = _dot(hi, suffix) + _dot(lo, suffix)
        tail = incl - l1m + carry
        w = jnp.where(ok_c, jnp.exp(ls + tail), 0.0)
        pw = jnp.concatenate([p[:c0], w, p[c1:]], axis=0).astype(BF16)
        acc_ref[...] = jnp.where(is_c, 1.0, alpha) * acc_ref[...] + pv(pw)
        m_ref[...] = jnp.broadcast_to(m_new, m_ref.shape)
        l_ref[...] = jnp.broadcast_to(l_new, l_ref.shape)
        cc_ref[...] = jnp.broadcast_to(carry + incl[:, 0:1], cc_ref.shape)

    @pl.when(c == 0)
    def _():
        m_ref[...] = jnp.full(m_ref.shape, NEG, F32)
        l_ref[...] = jnp.zeros_like(l_ref)
        cc_ref[...] = jnp.zeros_like(cc_ref)
        acc_ref[...] = jnp.zeros_like(acc_ref)
        kn, vn = kn_ref[0], vn_ref[0]
        s = _dot_nt(q, kn)
        jn = _iota((N_QROWS, PAGE), 1)
        tn = _iota((N_QROWS, PAGE), 0) % T8
        live = jn < dec_seq
        sel_g = (_iota((8, LANES), 1) == _iota((8, LANES), 0) + F_LANE).astype(BF16)
        cn_pad = jnp.concatenate([cn, jnp.zeros((PAGE - T8, LANES), F32)], axis=0)
        cnt = jnp.zeros((8, PAGE), F32)
        for piece in _split3(cn_pad):
            cnt = cnt + _dot_nt(sel_g, piece)
        blocks = [jnp.zeros((b0, PAGE), F32)]
        blocks += [seln_ref[0]] * N_GROUP_HEADS
        blocks += [jnp.zeros((c1 - c0, PAGE), F32)]
        for g in range(N_GROUP_HEADS):
            blocks.append(cn[:, F_LANE + g:F_LANE + g + 1] - cnt[g:g + 1, :])
        bias = jnp.where(live & (jn <= tn), jnp.concatenate(blocks, axis=0), NEG)
        ok_c = (live & (jn < tn))[c0:c1]
        suffix = (_iota((PAGE, PAGE), 0) >= _iota((PAGE, PAGE), 1)).astype(BF16)
        update(s, bias, ok_c, lambda pw: _dot(pw, vn), suffix)

    suffix = (_iota((CK, CK), 0) >= _iota((CK, CK), 1)).astype(BF16)
    for half in reversed(range(DEC_HALVES)):
        kt = jnp.concatenate([r[0, 0].astype(BF16) for r in k_refs[half * CP:(half + 1) * CP]], axis=1)
        vt = jnp.concatenate([r[0, 0].astype(BF16) for r in v_refs[half * CP:(half + 1) * CP]], axis=1)
        s = _dot(q, kt)
        blocks = [jnp.zeros((b0, CK), F32)]
        blocks += [selp_ref[0, half]] * N_GROUP_HEADS
        blocks += [jnp.zeros((c1 - c0, CK), F32)]
        dsuf = dsuf_ref[0, half]
        for g in range(N_GROUP_HEADS):
            blocks.append(cn[:, F_LANE + g:F_LANE + g + 1] + dsuf[g:g + 1, :])
        bias = jnp.concatenate(blocks, axis=0)
        update(s, bias, jnp.full((c1 - c0, CK), True), lambda pw, vt=vt: _dot_nt(pw, vt), suffix)

    @pl.when(c == nch - 1)
    def _():
        lv = lamv_ref[...]
        lam = (jnp.exp(jnp.sum(lv[0:1] * lv[1:2], axis=1, keepdims=True))
               - jnp.exp(jnp.sum(lv[2:3] * lv[3:4], axis=1, keepdims=True))) + lam_init
        o_all = acc_ref[...] / jnp.where(is_c, 1.0, l_ref[:, 0:1])
        lane = _iota((T8, N_HEADS * HEAD_DIM), 1)
        out = jnp.zeros((T8, N_HEADS * HEAD_DIM), F32)
        for hh in range(N_GROUP_HEADS):
            r1, r2 = _slot_rows(SLOT_A1, hh), _slot_rows(SLOT_A2, hh)
            oa = o_all[r1:r1 + T8] - lam * o_all[r2:r2 + T8]
            hm = (lane >= hh * HEAD_DIM) & (lane < (hh + 1) * HEAD_DIM)
            ms = jnp.sum(jnp.where(hm, oa * oa, 0.0), axis=1, keepdims=True) / HEAD_DIM
            oa = oa * lax.rsqrt(ms + LN_EPS) * subg_ref[...] * (1.0 - lam_init)
            out = jnp.where(hm, oa, out)
            for grp, slot in ((1, SLOT_B), (2, SLOT_C), (3, SLOT_D)):
                r = _slot_rows(slot, hh)
                head = grp * N_GROUP_HEADS + hh
                hm = (lane >= head * HEAD_DIM) & (lane < (head + 1) * HEAD_DIM)
                out = jnp.where(hm, o_all[r:r + T8], out)
        o_ref[0] = out.astype(o_ref.dtype)


def _dec_attention(l, lam_init, dec_seq, pt_flat, cache_k, cache_v, qall, k_new, v_new, selp, seln, dsuf,
                   misc_new, lamv, subg16, n_pages):
    db = qall.shape[0]
    n_pg = DEC_HALVES * CP
    nch = n_pages // n_pg
    mix = N_HEADS * HEAD_DIM

    def page_spec(u):
        return pl.BlockSpec((1, 1, mix, PAGE),
                            lambda b, c, pt: (l, pt[b * n_pages + (nch - 1 - c) * n_pg + u], 0, 0))

    per_b = lambda b, c, pt: (b, 0, 0)
    chunk = lambda b, c, pt: (b, nch - 1 - c, 0, 0)
    fixed = lambda b, c, pt: (0, 0)
    return pl.pallas_call(
        functools.partial(_dec_attn_kernel, lam_init, dec_seq),
        grid_spec=pltpu.PrefetchScalarGridSpec(
            num_scalar_prefetch=1,
            grid=(db, nch),
            in_specs=[page_spec(u) for u in range(n_pg)] * 2 + [
                pl.BlockSpec((1, N_QROWS, mix), per_b),
                pl.BlockSpec((1, PAGE, mix), per_b), pl.BlockSpec((1, PAGE, mix), per_b),
                pl.BlockSpec((1, DEC_HALVES, T8, CK), chunk), pl.BlockSpec((1, T8, PAGE), per_b),
                pl.BlockSpec((1, DEC_HALVES, 8, CK), chunk), pl.BlockSpec((1, T8, LANES), per_b),
                pl.BlockSpec((8, LANES), fixed), pl.BlockSpec((1, mix), fixed)],
            out_specs=pl.BlockSpec((1, T8, mix), per_b),
            scratch_shapes=[pltpu.VMEM((N_QROWS, LANES), F32), pltpu.VMEM((N_QROWS, LANES), F32),
                            pltpu.VMEM((_slot_rows(SLOT_D, 0) - _slot_rows(SLOT_C, 0), LANES), F32),
                            pltpu.VMEM((N_QROWS, mix), F32)]),
        out_shape=jax.ShapeDtypeStruct((db, T8, mix), BF16),
        compiler_params=_cparams(("parallel", "arbitrary")),
        name="dec_attn",
    )(pt_flat, *([cache_k] * n_pg), *([cache_v] * n_pg), qall, k_new, v_new, selp, seln, dsuf, misc_new, lamv,
      subg16)


def _slot_lane_masks():
    mix = N_HEADS * HEAD_DIM
    m = np.zeros((5 * N_GROUP_HEADS, mix), np.float32)
    for hh in range(N_GROUP_HEADS):
        m[SLOT_A1 * 4 + hh, hh * HEAD_DIM:hh * HEAD_DIM + DIFF_HALF] = 1
        m[SLOT_A2 * 4 + hh, hh * HEAD_DIM + DIFF_HALF:(hh + 1) * HEAD_DIM] = 1
        for grp, slot in ((1, SLOT_B), (2, SLOT_C), (3, SLOT_D)):
            head = grp * N_GROUP_HEADS + hh
            m[slot * 4 + hh, head * HEAD_DIM:(head + 1) * HEAD_DIM] = 1
    return m


def _sample_attention(l, lam_init, p, rows, lamv, subg2, db, dec_seq):
    qs, kb, vb, qib, misc = rows
    n_s = db * dec_seq
    mix = N_HEADS * HEAD_DIM
    page_table = p["page_table"]
    n_pages = page_table.shape[1]
    pt_flat = page_table.reshape(-1).astype(I32)
    n_keys = n_pages * PAGE + dec_seq
    n_sel = min(DSA_TOPK, n_keys // 4)

    def per_seq(a, pad_to):
        a = a[:n_s].reshape(db, dec_seq, a.shape[-1])
        return jnp.pad(a, ((0, 0), (0, pad_to - dec_seq), (0, 0)))

    misc8 = per_seq(misc, T8)
    qi8 = per_seq(qib, T8).reshape(db, T8, IDX_HEADS, IDX_DIM)
    qim = jnp.swapaxes(qi8, 1, 2).reshape(db, IDX_HEADS * T8, IDX_DIM)
    wv = jnp.swapaxes(misc8[:, :, WI_LANE:WI_LANE + IDX_HEADS], 1, 2).reshape(db, IDX_HEADS * T8, 1)
    wv = jnp.broadcast_to(wv, (db, IDX_HEADS * T8, LANES))
    ki_new = per_seq(misc, PAGE)[:, :, :IDX_DIM]
    selp, seln, dsuf = _dec_index(l, n_sel, dec_seq, pt_flat, p["cache_idx"], p["cache_logf"], qim, wv,
                                  ki_new, n_pages)

    masks = jnp.asarray(_slot_lane_masks(), BF16)
    q8 = per_seq(qs, T8)
    qall = (q8[:, None, :, :] * masks[None, :, None, :]).reshape(db, N_QROWS, mix)
    subg16 = jnp.tile(subg2, (1, mix // LANES))
    o8 = _dec_attention(l, lam_init, dec_seq, pt_flat, p["cache_k"], p["cache_v"], qall,
                        per_seq(kb, PAGE), per_seq(vb, PAGE), selp, seln, dsuf, misc8, lamv, subg16, n_pages)
    return o8[:, :dec_seq].reshape(n_s, mix)


E_LANE = N_EXPERT_GROUPS
BIG_LANE = 4096


def _layer_norm(x, g, b):
    mu = jnp.mean(x, axis=1, keepdims=True)
    xc = x - mu
    var = jnp.mean(xc * xc, axis=1, keepdims=True)
    return xc * lax.rsqrt(var + LN_EPS) * g + b


def _first_lane(mask, lane):
    return jnp.min(jnp.where(mask, lane, BIG_LANE), axis=1, keepdims=True)


def _mix_out_kernel(alpha, oa_ref, ob_ref, oc_ref, od_ref, x_ref, wo_ref, g_ref, b_ref, wr_ref, rb_ref,
                    h_ref, route_ref, cnt_ref, carry_ref):
    t = pl.program_id(0)
    grp_w = N_GROUP_HEADS * HEAD_DIM
    acc = x_ref[...] * alpha
    for n, o_ref in enumerate((oa_ref, ob_ref, oc_ref, od_ref)):
        acc = acc + _dot(o_ref[...], wo_ref[n * grp_w:(n + 1) * grp_w, :])
    h = _layer_norm(acc, g_ref[...], b_ref[...])
    h_ref[...] = h

    h_hi, h_lo = _split2(h)
    lg = (_dot(h_hi, wr_ref[0]) + _dot(h_lo, wr_ref[0]) + _dot(h_hi, wr_ref[1])) + rb_ref[...]
    lane = _iota(lg.shape, 1)
    is_g = lane < E_LANE
    is_e = (lane >= E_LANE) & (lane < E_LANE + N_EXPERTS)
    gmax = jnp.max(jnp.where(is_g, lg, NEG), axis=1, keepdims=True)
    grp = _first_lane(is_g & (lg == gmax), lane)
    pg_top = 1.0 / jnp.sum(jnp.where(is_g, jnp.exp(lg - gmax), 0.0), axis=1, keepdims=True)
    in_grp = is_e & (((lane - E_LANE) // EXPERTS_PER_GROUP) == grp)
    emax = jnp.max(jnp.where(in_grp, lg, NEG), axis=1, keepdims=True)
    ex = jnp.where(in_grp, jnp.exp(lg - emax), 0.0)
    pe = ex / jnp.sum(ex, axis=1, keepdims=True)
    p0 = jnp.max(jnp.where(in_grp, pe, -1.0), axis=1, keepdims=True)
    i0 = _first_lane(in_grp & (pe == p0), lane)
    rest = in_grp & (lane != i0)
    p1 = jnp.max(jnp.where(rest, pe, -1.0), axis=1, keepdims=True)
    i1 = _first_lane(rest & (pe == p1), lane)
    g0 = pg_top * p0 / (p0 + p1)
    g1 = pg_top * p1 / (p0 + p1)

    @pl.when(t == 0)
    def _():
        carry_ref[...] = jnp.zeros_like(carry_ref)
    onehot = jnp.where((lane == i0) | (lane == i1), 1.0, 0.0)
    r = _iota((TM, TM), 0)
    c = _iota((TM, TM), 1)
    before = (c < r).astype(BF16)
    prefix = _dot(before, onehot.astype(BF16)) + carry_ref[0:1, :]
    rank0 = jnp.sum(jnp.where(lane == i0, prefix, 0.0), axis=1, keepdims=True)
    rank1 = jnp.sum(jnp.where(lane == i1, prefix, 0.0), axis=1, keepdims=True)
    total = carry_ref[0:1, :] + jnp.sum(onehot, axis=0, keepdims=True)
    carry_ref[...] = jnp.broadcast_to(total, carry_ref.shape)
    cnt_ref[...] = jnp.broadcast_to(total, cnt_ref.shape)

    vals = (i0.astype(F32) - E_LANE, i1.astype(F32) - E_LANE, g0, g1, rank0, rank1)
    route = jnp.zeros(lg.shape, F32)
    for n, val in enumerate(vals):
        route = jnp.where(lane == n, val, route)
    route_ref[...] = route


def _mix_out(alpha, o4, x, w_out, ln_g, ln_b, wr, rb):
    tp, d = x.shape
    grp_w = N_GROUP_HEADS * HEAD_DIM
    row = lambda t: (t, 0)
    fixed = lambda t: (0, 0)
    return pl.pallas_call(
        functools.partial(_mix_out_kernel, alpha),
        grid=(tp // TM,),
        in_specs=[pl.BlockSpec((TM, grp_w), row)] * 4 + [
            pl.BlockSpec((TM, d), row), pl.BlockSpec(w_out.shape, fixed),
            pl.BlockSpec((1, d), fixed), pl.BlockSpec((1, d), fixed),
            pl.BlockSpec(wr.shape, lambda t: (0, 0, 0)), pl.BlockSpec((1, LANES), fixed)],
        out_specs=(pl.BlockSpec((TM, d), row), pl.BlockSpec((TM, LANES), row), pl.BlockSpec((8, LANES), fixed)),
        out_shape=(jax.ShapeDtypeStruct((tp, d), F32), jax.ShapeDtypeStruct((tp, LANES), F32),
                   jax.ShapeDtypeStruct((8, LANES), F32)),
        scratch_shapes=[pltpu.VMEM((8, LANES), F32)],
        compiler_params=_cparams(("arbitrary",)),
        name="mix_out",
    )(*o4, x, w_out, ln_g, ln_b, wr, rb)


def _dispatch_kernel(dest_ref, h_ref, buf_in_ref, buf_ref, sem):
    del buf_in_ref
    t = pl.program_id(0)

    def copy(r, k):
        d = dest_ref[(t * TM + r) * 2 + k]
        return pltpu.make_async_copy(h_ref.at[pl.ds(r, 1), :], buf_ref.at[pl.ds(d, 1), :], sem)

    def start(r, _):
        copy(r, 0).start()
        copy(r, 1).start()
        return 0

    def wait(r, _):
        copy(r, 0).wait()
        copy(r, 1).wait()
        return 0

    lax.fori_loop(0, TM, start, 0)
    lax.fori_loop(0, TM, wait, 0)


def _dispatch(dest_flat, h, n_rows):
    tp, d = h.shape
    buf0 = jnp.zeros((n_rows, d), F32)
    return pl.pallas_call(
        _dispatch_kernel,
        grid_spec=pltpu.PrefetchScalarGridSpec(
            num_scalar_prefetch=1,
            grid=(tp // TM,),
            in_specs=[pl.BlockSpec((TM, d), lambda t, dest: (t, 0)), pl.BlockSpec(memory_space=pl.ANY)],
            out_specs=pl.BlockSpec(memory_space=pl.ANY),
            scratch_shapes=[pltpu.SemaphoreType.DMA(())]),
        out_shape=jax.ShapeDtypeStruct((n_rows, d), F32),
        input_output_aliases={2: 0},
        compiler_params=_cparams(("arbitrary",)),
        name="moe_dispatch",
    )(dest_flat, h, buf0)


def _expert_kernel(be_ref, nb_ref, x_ref, wgu_ref, wd_ref, o_ref):
    j = pl.program_id(0)

    @pl.when(j < nb_ref[0])
    def _():
        de = wd_ref.shape[1]
        xb = x_ref[...].astype(BF16)
        hid = _dot(xb, wgu_ref[0].astype(BF16))
        a, u = hid[:, :de], hid[:, de:]
        act = (a * (1.0 / (1.0 + jnp.exp(-a)))) * u
        o_ref[...] = _dot(act.astype(BF16), wd_ref[0].astype(BF16))

    @pl.when(j >= nb_ref[0])
    def _():
        o_ref[...] = jnp.zeros_like(o_ref)


def _experts(block_e, n_used, buf, w_gate_up, w_down):
    n_rows, d = buf.shape
    _, _, de2 = w_gate_up.shape
    return pl.pallas_call(
        _expert_kernel,
        grid_spec=pltpu.PrefetchScalarGridSpec(
            num_scalar_prefetch=2,
            grid=(n_rows // MOE_ROWS,),
            in_specs=[pl.BlockSpec((MOE_ROWS, d), lambda j, be, nb: (j, 0)),
                      pl.BlockSpec((1, d, de2), lambda j, be, nb: (be[j], 0, 0)),
                      pl.BlockSpec((1, de2 // 2, d), lambda j, be, nb: (be[j], 0, 0))],
            out_specs=pl.BlockSpec((MOE_ROWS, d), lambda j, be, nb: (j, 0))),
        out_shape=jax.ShapeDtypeStruct((n_rows, d), F32),
        compiler_params=_cparams(("arbitrary",)),
        name="moe_experts",
    )(block_e, n_used, buf, w_gate_up, w_down)


def _combine_kernel(alpha, dest_ref, h_ref, route_ref, g_ref, b_ref, ebuf_ref, y_ref, rows_ref, sem):
    t = pl.program_id(0)

    def copy(r, k):
        d = dest_ref[(t * TM + r) * 2 + k]
        return pltpu.make_async_copy(ebuf_ref.at[pl.ds(d, 1), :], rows_ref.at[k, pl.ds(r, 1), :], sem)

    def start(r, _):
        copy(r, 0).start()
        copy(r, 1).start()
        return 0

    def wait(r, _):
        copy(r, 0).wait()
        copy(r, 1).wait()
        return 0

    lax.fori_loop(0, TM, start, 0)
    lax.fori_loop(0, TM, wait, 0)
    route = route_ref[...]
    f = rows_ref[0] * route[:, 2:3] + rows_ref[1] * route[:, 3:4]
    y_ref[...] = _layer_norm(h_ref[...] * alpha + f, g_ref[...], b_ref[...])


def _combine(alpha, dest_flat, h, route, ln_g, ln_b, ebuf):
    tp, d = h.shape
    row = lambda t, dest: (t, 0)
    fixed = lambda t, dest: (0, 0)
    return pl.pallas_call(
        functools.partial(_combine_kernel, alpha),
        grid_spec=pltpu.PrefetchScalarGridSpec(
            num_scalar_prefetch=1,
            grid=(tp // TM,),
            in_specs=[pl.BlockSpec((TM, d), row), pl.BlockSpec((TM, LANES), row),
                      pl.BlockSpec((1, d), fixed), pl.BlockSpec((1, d), fixed),
                      pl.BlockSpec(memory_space=pl.ANY)],
            out_specs=pl.BlockSpec((TM, d), row),
            scratch_shapes=[pltpu.VMEM((2, TM, d), F32), pltpu.SemaphoreType.DMA(())]),
        out_shape=jax.ShapeDtypeStruct((tp, d), F32),
        compiler_params=_cparams(("arbitrary",)),
        name="moe_combine",
    )(dest_flat, h, route, ln_g, ln_b, ebuf)


def _moe(alpha, h, route, counts, w_gate_up, w_down, ln_g, ln_b):
    tp, _ = h.shape
    eid = route[:, 0:2].astype(I32)
    rank = route[:, 4:6].astype(I32)
    counts = counts.astype(I32)
    padded = (counts + MOE_ROWS - 1) // MOE_ROWS * MOE_ROWS
    pends = jnp.cumsum(padded)
    pstarts = pends - padded
    dest = (pstarts[eid] + rank).reshape(-1)
    n_blocks = -(-(tp * 2) // MOE_ROWS) + N_EXPERTS
    block_start = jnp.arange(n_blocks, dtype=I32) * MOE_ROWS
    block_e = jnp.minimum(jnp.sum(block_start[:, None] >= pends[None, :], axis=-1), N_EXPERTS - 1).astype(I32)
    n_used = (pends[-1:] // MOE_ROWS).astype(I32)
    buf = _dispatch(dest, h, n_blocks * MOE_ROWS)
    ebuf = _experts(block_e, n_used, buf, w_gate_up, w_down)
    return _combine(alpha, dest, h, route, ln_g, ln_b, ebuf)


def _round_up(n, m):
    return -(-n // m) * m


def _concat_rows(x_prompt, x_sample):
    d = x_prompt.shape[-1]
    xp = x_prompt.reshape(-1, d)
    xs = x_sample.reshape(-1, d)
    n = xp.shape[0] + xs.shape[0]
    pad = _round_up(n, TM) - n
    return jnp.concatenate([xp, xs, jnp.zeros((pad, d), xp.dtype)], axis=0)


def _tables(seq, dec_seq, past_len, dec_batch):
    n_s = dec_batch * dec_seq
    pos_s = past_len + (jnp.arange(_round_up(n_s, TM), dtype=I32) % dec_seq)
    return jnp.concatenate([_rope_tables(jnp.arange(seq, dtype=I32)), _rope_tables(pos_s)], axis=0)


def _prep_layer(p, l):
    w_in = p["w_in"][l]
    cols = w_in.shape[1]
    w_pad = jnp.pad(w_in, ((0, 0), (0, _round_up(cols, LANES) - cols))).astype(BF16)
    bf_row = jnp.zeros((1, LANES), F32).at[0, F_LANE:F_LANE + N_GROUP_HEADS].set(p["b_f"][l])
    lamv = jnp.zeros((8, LANES), F32)
    for n, name in enumerate(("lam_q1", "lam_k1", "lam_q2", "lam_k2")):
        lamv = lamv.at[n, :DIFF_HALF].set(p[name][l])
    subg2 = jnp.concatenate([p["sub_g"][l], p["sub_g"][l]])[None, :]
    wr = jnp.zeros((p["router_g"].shape[1], LANES), F32)
    wr = wr.at[:, :E_LANE].set(p["router_g"][l]).at[:, E_LANE:E_LANE + N_EXPERTS].set(p["router_e"][l])
    wr_hi = wr.astype(BF16)
    wr_lo = (wr - wr_hi.astype(F32)).astype(BF16)
    rb = jnp.zeros((1, LANES), F32)
    rb = rb.at[0, :E_LANE].set(p["router_g_b"][l]).at[0, E_LANE:E_LANE + N_EXPERTS].set(p["router_e_b"][l])
    return dict(w_in=w_pad, bf_row=bf_row, lamv=lamv, subg2=subg2, wr=jnp.stack([wr_hi, wr_lo]), rb=rb,
                w_out=p["w_out"][l].astype(BF16))


def kernel(x_prompt, x_sample, cache_k, cache_v, cache_idx, cache_logf, page_table, w_in, b_f, lam_q1, lam_k1,
           lam_q2, lam_k2, sub_g, w_out, ln1_g, ln1_b, router_g, router_g_b, router_e, router_e_b, w_gate_up,
           w_down, ln2_g, ln2_b):
    p = dict(cache_k=cache_k, cache_v=cache_v, cache_idx=cache_idx, cache_logf=cache_logf,
             page_table=page_table, w_in=w_in, b_f=b_f, lam_q1=lam_q1, lam_k1=lam_k1, lam_q2=lam_q2,
             lam_k2=lam_k2, sub_g=sub_g, w_out=w_out, router_g=router_g, router_g_b=router_g_b,
             router_e=router_e, router_e_b=router_e_b)
    depth = w_in.shape[0]
    n_seq, seq, d = x_prompt.shape
    db, dec_seq, _ = x_sample.shape
    t, n_s = n_seq * seq, db * dec_seq
    assert seq % TQ == 0 and t % TM == 0 and n_s <= TM and dec_seq <= T8
    assert page_table.shape[1] % (CP * DEC_HALVES) == 0 and cache_k.shape[2] == PAGE
    alpha = (2 * depth) ** 0.25
    past_len = page_table.shape[1] * PAGE
    tabs = _tables(seq, dec_seq, past_len, db)
    mix = N_HEADS * HEAD_DIM
    p["cache_k"] = jnp.transpose(cache_k, (0, 1, 3, 4, 2)).reshape(cache_k.shape[0], cache_k.shape[1], mix, PAGE)
    p["cache_v"] = jnp.transpose(cache_v, (0, 1, 3, 4, 2)).reshape(cache_v.shape[0], cache_v.shape[1], mix, PAGE)
    p["cache_idx"] = jnp.swapaxes(cache_idx, 2, 3)
    p["cache_logf"] = jnp.pad(jnp.swapaxes(cache_logf, 2, 3), ((0, 0), (0, 0), (0, 8 - N_GROUP_HEADS), (0, 0)))

    y = _concat_rows(x_prompt, x_sample)
    tp = y.shape[0]
    rows_p, rows_s = [], []
    mix = N_HEADS * HEAD_DIM
    n_pt, n_st = t // TM, (tp - t) // TM
    tiles_per_seq = seq // TM
    kv_p = (jnp.zeros((depth, t, mix), F32), jnp.zeros((depth, t, mix), F32))
    kv_s = (jnp.zeros((depth, tp - t, mix), F32), jnp.zeros((depth, tp - t, mix), F32))
    for l in range(depth):
        lam_init = 0.8 - 0.6 * math.exp(-0.3 * l)
        st = _prep_layer(p, l)
        qs, kb, _, kf, vf, qib, misc, ki2b, vx = _project(
            y, st["w_in"], tabs, st["bf_row"], 0, n_pt, lambda i: (i % tiles_per_seq, 0), l, depth, kv_p)
        kv_p = (kf, vf)
        qs_s, kb_s, vb_s, kf_s, vf_s, qib_s, misc_s, _, _ = _project(
            y, st["w_in"], tabs, st["bf_row"], n_pt, n_st, lambda i: (tiles_per_seq + i, 0), l, depth, kv_s)
        kv_s = (kf_s, vf_s)
        fcol, ft = _fcum(misc, n_seq, seq)
        o_p = _prompt_attention(lam_init, qs, kb, vx, qib, ki2b, misc, fcol, ft, st["lamv"], st["subg2"],
                                n_seq, seq)
        o_s = _sample_attention(l, lam_init, p, (qs_s, kb_s, vb_s, qib_s, misc_s),
                                st["lamv"], st["subg2"], db, dec_seq)
        grp_w = N_GROUP_HEADS * HEAD_DIM
        o4 = [jnp.concatenate([o_p[g], o_s[:, g * grp_w:(g + 1) * grp_w],
                               jnp.zeros((tp - t - n_s, grp_w), BF16)], axis=0) for g in range(4)]
        h, route, cnt = _mix_out(alpha, o4, y, st["w_out"], ln1_g[l][None, :], ln1_b[l][None, :],
                                 st["wr"], st["rb"])
        y = _moe(alpha, h, route, cnt[0, E_LANE:E_LANE + N_EXPERTS], w_gate_up[l], w_down[l],
                 ln2_g[l][None, :], ln2_b[l][None, :])
        rows_p.append((misc[:, :IDX_DIM], misc[:, F_LANE:F_LANE + N_GROUP_HEADS]))
        rows_s.append((misc_s[:n_s, :IDX_DIM], misc_s[:n_s, F_LANE:F_LANE + N_GROUP_HEADS]))

    def stack(rows, n, lead, tail):
        return jnp.stack([r[n].reshape(*lead, *tail) for r in rows])

    outs = [y[:t].reshape(n_seq, seq, d), y[t:t + n_s].reshape(db, dec_seq, d)]
    outs += [a.reshape(depth, n_seq, seq, N_HEADS, HEAD_DIM) for a in kv_p]
    outs += [stack(rows_p, 0, (n_seq, seq), (IDX_DIM,)), stack(rows_p, 1, (n_seq, seq), (N_GROUP_HEADS,))]
    outs += [a[:, :n_s].reshape(depth, db, dec_seq, N_HEADS, HEAD_DIM) for a in kv_s]
    outs += [stack(rows_s, 0, (db, dec_seq), (IDX_DIM,)), stack(rows_s, 1, (db, dec_seq), (N_GROUP_HEADS,))]
    return tuple(outs)
```
